```python
import math
import jax, jax.numpy as jnp
from jax import lax
import numpy as np

D_MODEL = 2048
BATCH = 1
SEQ = 16384
DEPTH = 2

N_META = 16
N_A_LAYERS = DEPTH // 2
N_B_LAYERS = DEPTH - N_A_LAYERS
GLA_HEADS = 4
GLA_DK = D_MODEL // 2 // GLA_HEADS
GLA_DV = D_MODEL // GLA_HEADS
GLA_GATE_RANK = 16
GLA_GATE_TAU = 16.0
GLA_CHUNK = 64
GLA_QK = GLA_HEADS * GLA_DK
GLA_VD = GLA_HEADS * GLA_DV
GLA_IN = 2 * GLA_QK + 2 * GLA_VD + GLA_GATE_RANK
SB_HEADS = 16
SB_HEAD_DIM = D_MODEL // SB_HEADS
SB_BLOCK = 128
FFN_HIDDEN = ((8 * D_MODEL // 3 + 255) // 256) * 256
DN_ALPHA = (2.0 * DEPTH) ** 0.25
DN_BETA = (8.0 * DEPTH) ** -0.25
LN_EPS = 1e-5
RMS_EPS = 1e-6

kernel_name = "yoco_gla_stickbreaking_macaron_deepnorm"


def layer_norm(x, g, b):
    xf = x.astype(jnp.float32)
    mu = jnp.mean(xf, -1, keepdims=True)
    var = jnp.mean(jnp.square(xf - mu), -1, keepdims=True)
    return ((xf - mu) * lax.rsqrt(var + LN_EPS) * g + b).astype(x.dtype)


def swiglu(x, w_gu, w_down):
    gu = x @ w_gu
    g, u = gu[..., :FFN_HIDDEN], gu[..., FFN_HIDDEN:]
    return (jax.nn.silu(g) * u) @ w_down


def gla_chunk(state, inputs):
    q, k, v, lg = inputs
    C = q.shape[1]
    b = jnp.cumsum(lg, axis=1)
    o_inter = jnp.einsum('bthk,bhkv->bthv', q * jnp.exp(b), state)
    causal = jnp.tril(jnp.ones((C, C), bool))
    diff = b[:, :, None] - b[:, None, :]
    decay = jnp.exp(jnp.where(causal[None, :, :, None, None], diff, -jnp.inf))
    scores = jnp.einsum('bthk,bshk,btshk->bhts', q, k, decay)
    o_intra = jnp.einsum('bhts,bshv->bthv', scores, v)
    b_last = b[:, -1]
    k_dec = k * jnp.exp(b_last[:, None] - b)
    new_state = jnp.exp(b_last)[..., None] * state + jnp.einsum('bshk,bshv->bhkv', k_dec, v)
    return new_state, o_inter + o_intra


def gla_mixer(h, w_in, w_g2, b_g2, norm_g, w_out):
    B, L, _ = h.shape
    proj = (h @ w_in).astype(jnp.float32)
    q = proj[..., :GLA_QK].reshape(B, L, GLA_HEADS, GLA_DK) * (GLA_DK ** -0.5)
    k = proj[..., GLA_QK:2 * GLA_QK].reshape(B, L, GLA_HEADS, GLA_DK)
    v = proj[..., 2 * GLA_QK:2 * GLA_QK + GLA_VD].reshape(B, L, GLA_HEADS, GLA_DV)
    r = proj[..., 2 * GLA_QK + GLA_VD:2 * GLA_QK + 2 * GLA_VD]
    g_low = proj[..., 2 * GLA_QK + 2 * GLA_VD:]
    lg = jax.nn.log_sigmoid(g_low @ w_g2.astype(jnp.float32) + b_g2.astype(jnp.float32)) / GLA_GATE_TAU
    lg = lg.reshape(B, L, GLA_HEADS, GLA_DK)
    s0 = jnp.zeros((B, GLA_HEADS, GLA_DK, GLA_DV), jnp.float32)
    s1, o_meta = gla_chunk(s0, (q[:, :N_META], k[:, :N_META], v[:, :N_META], lg[:, :N_META]))

    def to_chunks(t):
        return jnp.moveaxis(t[:, N_META:].reshape(B, -1, GLA_CHUNK, *t.shape[2:]), 1, 0)

    _, o_real = lax.scan(gla_chunk, s1, (to_chunks(q), to_chunks(k), to_chunks(v), to_chunks(lg)))
    o_real = jnp.moveaxis(o_real, 0, 1).reshape(B, L - N_META, GLA_HEADS, GLA_DV)
    o = jnp.concatenate([o_meta, o_real], axis=1)
    o = o * lax.rsqrt(jnp.mean(jnp.square(o), -1, keepdims=True) + RMS_EPS) * norm_g.astype(jnp.float32)
    o = o.reshape(B, L, GLA_VD) * jax.nn.silu(r)
    return o.astype(h.dtype) @ w_out


def sb_block(q_blk, q_pos, k, v):
    z = jnp.einsum('bqhd,bkhd->bhqk', q_blk, k)
    key_pos = jnp.arange(k.shape[1])
    strict = (key_pos[None, :] < q_pos[:, None])[None, None]
    log_1m = jnp.where(strict, jax.nn.log_sigmoid(-z), 0.0)
    tail = lax.cumsum(log_1m, axis=3, reverse=True) - log_1m
    w = jnp.where(strict, jnp.exp(jax.nn.log_sigmoid(z) + tail), 0.0)
    return jnp.einsum('bhqk,bkhd->bqhd', w, v)


def sb_mixer(h, w_q, w_out, k, v):
    B, L, _ = h.shape
    q = (h @ w_q).astype(jnp.float32).reshape(B, L, SB_HEADS, SB_HEAD_DIM) * (SB_HEAD_DIM ** -0.5)
    o_meta = sb_block(q[:, :N_META], jnp.arange(N_META), k, v)
    n_blk = (L - N_META) // SB_BLOCK
    q_blocks = jnp.moveaxis(q[:, N_META:].reshape(B, n_blk, SB_BLOCK, SB_HEADS, SB_HEAD_DIM), 1, 0)
    pos_blocks = (N_META + jnp.arange(L - N_META)).reshape(n_blk, SB_BLOCK)
    o_real = lax.map(lambda a: sb_block(a[0], a[1], k, v), (q_blocks, pos_blocks))
    o_real = jnp.moveaxis(o_real, 0, 1).reshape(B, L - N_META, SB_HEADS, SB_HEAD_DIM)
    o = jnp.concatenate([o_meta, o_real], axis=1).reshape(B, L, D_MODEL)
    return o.astype(h.dtype) @ w_out


def setup_inputs(seed: int = 0) -> dict:
    key = jax.random.key(seed)
    ks = jax.random.split(key, 16)
    nrm = jax.random.normal
    f32 = jnp.float32
    x = nrm(ks[0], (BATCH, SEQ, D_MODEL), f32)
    meta_tokens = nrm(ks[1], (N_META, D_MODEL), f32)
    ffn_w_gu = nrm(ks[2], (DEPTH, 2, D_MODEL, 2 * FFN_HIDDEN), f32) * (D_MODEL ** -0.5)
    ffn_w_down = nrm(ks[3], (DEPTH, 2, FFN_HIDDEN, D_MODEL), f32) * (FFN_HIDDEN ** -0.5) * DN_BETA
    ln_g = 1.0 + 0.02 * nrm(ks[4], (DEPTH, 3, D_MODEL), f32)
    ln_b = 0.02 * nrm(ks[5], (DEPTH, 3, D_MODEL), f32)
    col_scale = jnp.ones((GLA_IN,), f32).at[2 * GLA_QK:2 * GLA_QK + GLA_VD].set(DN_BETA)
    gla_w_in = nrm(ks[6], (N_A_LAYERS, D_MODEL, GLA_IN), f32) * (D_MODEL ** -0.5) * col_scale
    gla_w_g2 = nrm(ks[7], (N_A_LAYERS, GLA_GATE_RANK, GLA_QK), f32) * (GLA_GATE_RANK ** -0.5)
    gla_b_g2 = 0.1 * nrm(ks[8], (N_A_LAYERS, GLA_QK), f32)
    gla_norm_g = 1.0 + 0.02 * nrm(ks[9], (N_A_LAYERS, GLA_DV), f32)
    gla_w_out = nrm(ks[10], (N_A_LAYERS, GLA_VD, D_MODEL), f32) * (GLA_VD ** -0.5) * DN_BETA
    kv_scale = jnp.ones((2 * D_MODEL,), f32).at[D_MODEL:].set(DN_BETA)
    sb_w_kv = nrm(ks[11], (D_MODEL, 2 * D_MODEL), f32) * (D_MODEL ** -0.5) * kv_scale
    sb_w_q = nrm(ks[12], (N_B_LAYERS, D_MODEL, D_MODEL), f32) * (D_MODEL ** -0.5)
    sb_w_out = nrm(ks[13], (N_B_LAYERS, D_MODEL, D_MODEL), f32) * (D_MODEL ** -0.5) * DN_BETA
    return {"x": x, "meta_tokens": meta_tokens, "ffn_w_gu": ffn_w_gu, "ffn_w_down": ffn_w_down,
            "ln_g": ln_g, "ln_b": ln_b, "gla_w_in": gla_w_in, "gla_w_g2": gla_w_g2,
            "gla_b_g2": gla_b_g2, "gla_norm_g": gla_norm_g, "gla_w_out": gla_w_out,
            "sb_w_kv": sb_w_kv, "sb_w_q": sb_w_q, "sb_w_out": sb_w_out}


def reference(x, meta_tokens, ffn_w_gu, ffn_w_down, ln_g, ln_b, gla_w_in, gla_w_g2, gla_b_g2,
              gla_norm_g, gla_w_out, sb_w_kv, sb_w_q, sb_w_out):
    B = x.shape[0]
    meta = jnp.broadcast_to(meta_tokens[None].astype(x.dtype), (B, N_META, D_MODEL))
    h = jnp.concatenate([meta, x], axis=1)
    L = h.shape[1]
    k_shared = None
    v_shared = None
    for layer in range(DEPTH):
        h = layer_norm(DN_ALPHA * h + 0.5 * swiglu(h, ffn_w_gu[layer, 0], ffn_w_down[layer, 0]),
                       ln_g[layer, 0], ln_b[layer, 0])
        if layer < N_A_LAYERS:
            mix = gla_mixer(h, gla_w_in[layer], gla_w_g2[layer], gla_b_g2[layer],
                            gla_norm_g[layer], gla_w_out[layer])
        else:
            j = layer - N_A_LAYERS
            mix = sb_mixer(h, sb_w_q[j], sb_w_out[j], k_shared, v_shared)
        h = layer_norm(DN_ALPHA * h + mix, ln_g[layer, 1], ln_b[layer, 1])
        h = layer_norm(DN_ALPHA * h + 0.5 * swiglu(h, ffn_w_gu[layer, 1], ffn_w_down[layer, 1]),
                       ln_g[layer, 2], ln_b[layer, 2])
        if layer == N_A_LAYERS - 1:
            kv = (h @ sb_w_kv).astype(jnp.float32).reshape(B, L, 2, SB_HEADS, SB_HEAD_DIM)
            k_shared = kv[:, :, 0]
            v_shared = kv[:, :, 1]
    return h[:, N_META:]
```

```python
import functools
import math

import numpy as np
import jax
import jax.numpy as jnp
from jax import lax
from jax.experimental import pallas as pl
from jax.experimental.pallas import tpu as pltpu

F32 = jnp.float32
BF16 = jnp.bfloat16

N_META = 16
GLA_HEADS = 4
GLA_GATE_RANK = 16
GLA_GATE_TAU = 16.0
SB_HEADS = 16
LN_EPS = 1e-5
RMS_EPS = 1e-6

LANES = 128
VMEM_LIMIT_BYTES = 56 * 1024 * 1024

ROW_TILE = 640
ROW_ALIGN = 1280
FFN_HIDDEN_TILE = 512
PROJ_COL_TILE = 512
GLA_CHUNK = 128
SB_BLOCK = 256
SB_LOG_WEIGHT_FLOOR = -105.0


def _dot(a, b):
    return jnp.dot(a, b, preferred_element_type=F32)


def _dot_nt(a, b):
    return lax.dot_general(a, b, (((1,), (1,)), ((), ())), preferred_element_type=F32)


def _dot_tn(a, b):
    return lax.dot_general(a, b, (((0,), (0,)), ((), ())), preferred_element_type=F32)


def _log_sigmoid(x):
    return -(jnp.maximum(-x, 0.0) + jnp.log1p(jnp.exp(-jnp.abs(x))))


def _silu(x):
    return x * jax.nn.sigmoid(x)


def _layer_norm(y, g, b):
    mu = jnp.mean(y, -1, keepdims=True)
    yc = y - mu
    var = jnp.mean(yc * yc, -1, keepdims=True)
    return yc * lax.rsqrt(var + LN_EPS) * g + b


def _params(*sem):
    return pltpu.CompilerParams(dimension_semantics=sem, vmem_limit_bytes=VMEM_LIMIT_BYTES)


def _ffn_ln_kernel(h_ref, wg_ref, wu_ref, wd_ref, g_ref, b_ref, o_ref, ob_ref, xb_ref, acc_ref, *, alpha):
    j = pl.program_id(1)

    @pl.when(j == 0)
    def _():
        xb_ref[...] = h_ref[...].astype(BF16)
        acc_ref[...] = jnp.zeros_like(acc_ref)

    x = xb_ref[...]
    gate = _dot(x, wg_ref[...])
    up = _dot(x, wu_ref[...])
    act = (_silu(gate) * up).astype(BF16)
    acc_ref[...] += _dot(act, wd_ref[...])

    @pl.when(j == pl.num_programs(1) - 1)
    def _():
        y = _layer_norm(alpha * h_ref[...] + 0.5 * acc_ref[...], g_ref[...], b_ref[...])
        o_ref[...] = y
        ob_ref[...] = y.astype(BF16)


def _ffn_ln(h, w_gu, w_down, g, b, *, alpha, tm, th):
    rows, d = h.shape
    hidden = w_down.shape[0]
    nh = hidden // th
    assert rows % tm == 0 and hidden % th == 0 and w_gu.shape == (d, 2 * hidden)
    return pl.pallas_call(
        functools.partial(_ffn_ln_kernel, alpha=alpha),
        grid=(rows // tm, nh),
        in_specs=[
            pl.BlockSpec((tm, d), lambda i, j: (i, 0)),
            pl.BlockSpec((d, th), lambda i, j: (0, j)),
            pl.BlockSpec((d, th), lambda i, j: (0, j + nh)),
            pl.BlockSpec((th, d), lambda i, j: (j, 0)),
            pl.BlockSpec((1, d), lambda i, j: (0, 0)),
            pl.BlockSpec((1, d), lambda i, j: (0, 0)),
        ],
        out_specs=[
            pl.BlockSpec((tm, d), lambda i, j: (i, 0)),
            pl.BlockSpec((tm, d), lambda i, j: (i, 0)),
        ],
        out_shape=[jax.ShapeDtypeStruct((rows, d), F32), jax.ShapeDtypeStruct((rows, d), BF16)],
        scratch_shapes=[pltpu.VMEM((tm, d), BF16), pltpu.VMEM((tm, d), F32)],
        compiler_params=_params("parallel", "arbitrary"),
        name="ffn_ln",
    )(h, w_gu, w_gu, w_down, g.reshape(1, d), b.reshape(1, d))


def _proj_kernel(x_ref, w_ref, o_ref, *, scale):
    y = _dot(x_ref[...], w_ref[...])
    if scale != 1.0:
        y = y * scale
    o_ref[...] = y.astype(o_ref.dtype)


def _proj(x, w, *, out_dtype, tm, tn, scale=1.0):
    rows, k = x.shape
    n = w.shape[1]
    assert rows % tm == 0 and n % tn == 0 and w.shape[0] == k
    return pl.pallas_call(
        functools.partial(_proj_kernel, scale=scale),
        grid=(rows // tm, n // tn),
        in_specs=[
            pl.BlockSpec((tm, k), lambda i, j: (i, 0)),
            pl.BlockSpec((k, tn), lambda i, j: (0, j)),
        ],
        out_specs=pl.BlockSpec((tm, tn), lambda i, j: (i, j)),
        out_shape=jax.ShapeDtypeStruct((rows, n), out_dtype),
        compiler_params=_params("parallel", "arbitrary"),
        name="proj",
    )(x, w)


def _mix_ln_kernel(a_ref, w_ref, h_ref, g_ref, b_ref, o_ref, ob_ref, *, alpha):
    mix = _dot(a_ref[...], w_ref[...])
    y = _layer_norm(alpha * h_ref[...] + mix, g_ref[...], b_ref[...])
    o_ref[...] = y
    ob_ref[...] = y.astype(BF16)


def _mix_ln(a, w, h, g, b, *, alpha, tm):
    rows, d = h.shape
    k = a.shape[1]
    assert rows % tm == 0 and w.shape == (k, d)
    return pl.pallas_call(
        functools.partial(_mix_ln_kernel, alpha=alpha),
        grid=(rows // tm,),
        in_specs=[
            pl.BlockSpec((tm, k), lambda i: (i, 0)),
            pl.BlockSpec((k, d), lambda i: (0, 0)),
            pl.BlockSpec((tm, d), lambda i: (i, 0)),
            pl.BlockSpec((1, d), lambda i: (0, 0)),
            pl.BlockSpec((1, d), lambda i: (0, 0)),
        ],
        out_specs=[
            pl.BlockSpec((tm, d), lambda i: (i, 0)),
            pl.BlockSpec((tm, d), lambda i: (i, 0)),
        ],
        out_shape=[jax.ShapeDtypeStruct((rows, d), F32), jax.ShapeDtypeStruct((rows, d), BF16)],
        compiler_params=_params("parallel"),
        name="mix_ln",
    )(a, w, h, g.reshape(1, d), b.reshape(1, d))


def _gla_tables(c):
    n_lev = int(math.log2(c))
    assert 1 << n_lev == c
    t = np.arange(c)[:, None]
    j = np.arange(c)[None, :]
    blocks, masks = [], [np.eye(c, dtype=np.float32)]
    for lev in range(n_lev):
        m = c >> (lev + 1)
        mid = (t // (2 * m)) * (2 * m) + m
        upper = t >= mid
        a = np.where(upper, (j >= mid) & (j <= t), (j > t) & (j <= mid - 1))
        blocks.append(a)
        s_mid = (j // (2 * m)) * (2 * m) + m
        masks.append(((t // (2 * m)) == (j // (2 * m))) & upper & (j < s_mid))
    blocks.append(j <= t)
    blocks.append(j > t)
    sums = np.concatenate(blocks, axis=0).astype(np.float32)
    sums = np.concatenate([sums, sums], axis=1)
    return n_lev, jnp.asarray(sums, BF16), jnp.asarray(np.stack(masks).astype(np.float32), F32)


def _gla_kernel(q_ref, k_ref, v_ref, r_ref, gl_ref, wg2_ref, bg2_ref, ng_ref, sums_ref, masks_ref,
                o_ref, st_ref, *, heads, dk, dv, n_lev, q_scale):
    c = q_ref.shape[0]

    @pl.when(pl.program_id(0) == 0)
    def _():
        st_ref[...] = jnp.zeros_like(st_ref)

    g_low = gl_ref[...].astype(BF16)
    for hd in range(heads):
        ks = slice(hd * dk, (hd + 1) * dk)
        vs = slice(hd * dv, (hd + 1) * dv)
        x = _dot(g_low, wg2_ref[:, ks]) + bg2_ref[:, ks]
        lg = _log_sigmoid(x) * (1.0 / GLA_GATE_TAU)
        lg_hi = lg.astype(BF16)
        lg_lo = (lg - lg_hi.astype(F32)).astype(BF16)
        fac = jnp.exp(_dot(sums_ref[...], jnp.concatenate([lg_hi, lg_lo], axis=0)))
        q = q_ref[:, ks].astype(F32) * q_scale
        k = k_ref[:, ks].astype(F32)
        v = v_ref[:, vs]
        scores = masks_ref[0] * _dot_nt(q.astype(BF16), k.astype(BF16))
        for lev in range(n_lev):
            f = fac[lev * c:(lev + 1) * c]
            scores = scores + masks_ref[lev + 1] * _dot_nt((q * f).astype(BF16), (k * f).astype(BF16))
        f_cum = fac[n_lev * c:(n_lev + 1) * c]
        f_dec = fac[(n_lev + 1) * c:(n_lev + 2) * c]
        state_t = st_ref[hd]
        o = _dot(scores.astype(BF16), v) + _dot_nt((q * f_cum).astype(BF16), state_t.astype(BF16))
        st_ref[hd] = state_t * f_cum[c - 1:c, :] + _dot_tn(v, (k * f_dec).astype(BF16))
        o = o * lax.rsqrt(jnp.mean(o * o, -1, keepdims=True) + RMS_EPS) * ng_ref[...]
        r = r_ref[:, vs].astype(F32)
        o_ref[:, vs] = (o * _silu(r)).astype(o_ref.dtype)


def _gla(qkvr, g_low, w_g2, b_g2, norm_g, *, heads, dk, dv, chunk):
    rows = qkvr.shape[0]
    qk, vd = heads * dk, heads * dv
    assert rows % chunk == 0 and qkvr.shape[1] == 2 * qk + 2 * vd and vd == 2 * qk
    n_lev, sums, masks = _gla_tables(chunk)
    rank_pad = g_low.shape[1]
    return pl.pallas_call(
        functools.partial(_gla_kernel, heads=heads, dk=dk, dv=dv, n_lev=n_lev, q_scale=dk ** -0.5),
        grid=(rows // chunk,),
        in_specs=[
            pl.BlockSpec((chunk, qk), lambda i: (i, 0)),
            pl.BlockSpec((chunk, qk), lambda i: (i, 1)),
            pl.BlockSpec((chunk, vd), lambda i: (i, 1)),
            pl.BlockSpec((chunk, vd), lambda i: (i, 2)),
            pl.BlockSpec((chunk, rank_pad), lambda i: (i, 0)),
            pl.BlockSpec((rank_pad, qk), lambda i: (0, 0)),
            pl.BlockSpec((1, qk), lambda i: (0, 0)),
            pl.BlockSpec((1, dv), lambda i: (0, 0)),
            pl.BlockSpec(sums.shape, lambda i: (0, 0)),
            pl.BlockSpec(masks.shape, lambda i: (0, 0, 0)),
        ],
        out_specs=pl.BlockSpec((chunk, vd), lambda i: (i, 0)),
        out_shape=jax.ShapeDtypeStruct((rows, vd), BF16),
        scratch_shapes=[pltpu.VMEM((heads, dv, dk), F32)],
        compiler_params=_params("arbitrary"),
        name="gla",
    )(qkvr, qkvr, qkvr, qkvr, g_low, w_g2, b_g2.reshape(1, qk), norm_g.reshape(1, dv), sums, masks)


def _sb_tables(tk):
    j = np.arange(tk)[:, None]
    s = np.arange(tk)[None, :]
    tab = np.concatenate([(j >= s), np.ones((tk, LANES), bool)], axis=1)
    return jnp.asarray(tab.astype(np.float32), BF16)


def _sb_kernel(q_ref, k_ref, v_ref, tab_ref, o_ref, acc_ref, carry_ref):
    blk = q_ref.shape[0]
    qi = pl.program_id(1)
    q = q_ref[...]

    def tile(j, diagonal):
        start = pl.multiple_of(j * blk, blk)
        z = _dot_nt(q, k_ref[pl.ds(start, blk), :])
        lm = -(jnp.maximum(z, 0.0) + jnp.log1p(jnp.exp(-jnp.abs(z))))
        if diagonal:
            strict = (lax.broadcasted_iota(jnp.int32, (blk, blk), 1)
                      < lax.broadcasted_iota(jnp.int32, (blk, blk), 0))
            lm = jnp.where(strict, lm, 0.0)
        hi = lm.astype(BF16)
        lo = (lm - hi.astype(F32)).astype(BF16)
        sums = _dot(hi, tab_ref[...]) + _dot(lo, tab_ref[...])
        carry = carry_ref[...]
        logit = z + sums[:, :blk] + jnp.concatenate([carry] * (blk // LANES), axis=1)
        w = jnp.exp(logit)
        if diagonal:
            w = jnp.where(strict, w, 0.0)
        acc_ref[...] += _dot(w.astype(BF16), v_ref[pl.ds(start, blk), :])
        carry_ref[...] = carry + sums[:, blk:]

    acc_ref[...] = jnp.zeros_like(acc_ref)
    carry_ref[...] = jnp.zeros_like(carry_ref)
    tile(qi, True)

    def cond(state):
        j, top = state
        return jnp.logical_and(j >= 0, top > SB_LOG_WEIGHT_FLOOR)

    def body(state):
        j, _ = state
        tile(j, False)
        return j - 1, jnp.max(carry_ref[...])

    lax.while_loop(cond, body, (qi - 1, jnp.max(carry_ref[...])))
    o_ref[...] = acc_ref[...].astype(o_ref.dtype)


def _sb_attention(q, kv, *, heads, blk):
    rows, d = q.shape
    dh = d // heads
    assert rows % blk == 0 and dh % LANES == 0 and blk % LANES == 0 and kv.shape == (rows, 2 * d)
    tab = _sb_tables(blk)
    return pl.pallas_call(
        _sb_kernel,
        grid=(heads, rows // blk),
        in_specs=[
            pl.BlockSpec((blk, dh), lambda h, i: (i, h)),
            pl.BlockSpec((rows, dh), lambda h, i: (0, h)),
            pl.BlockSpec((rows, dh), lambda h, i: (0, h + heads)),
            pl.BlockSpec(tab.shape, lambda h, i: (0, 0)),
        ],
        out_specs=pl.BlockSpec((blk, dh), lambda h, i: (i, h)),
        out_shape=jax.ShapeDtypeStruct((rows, d), BF16),
        scratch_shapes=[pltpu.VMEM((blk, dh), F32), pltpu.VMEM((blk, LANES), F32)],
        compiler_params=_params("parallel", "arbitrary"),
        name="sb_attention",
    )(q, kv, kv, tab)


def _tiles(rows_padded):
    tm = ROW_TILE if rows_padded % ROW_TILE == 0 else SB_BLOCK
    return dict(tm=tm)


def kernel(x, meta_tokens, ffn_w_gu, ffn_w_down, ln_g, ln_b, gla_w_in, gla_w_g2, gla_b_g2, gla_norm_g,
           gla_w_out, sb_w_kv, sb_w_q, sb_w_out):
    batch, seq, d = x.shape
    assert batch == 1 and meta_tokens.shape == (N_META, d)
    depth = ffn_w_gu.shape[0]
    n_a = gla_w_in.shape[0]
    alpha = (2.0 * depth) ** 0.25
    rows = N_META + seq
    rows_p = -(-rows // ROW_ALIGN) * ROW_ALIGN
    tm = _tiles(rows_p)["tm"]

    qk = gla_w_g2.shape[2]
    vd = gla_w_out.shape[1]
    dk, dv = qk // GLA_HEADS, vd // GLA_HEADS
    dh = d // SB_HEADS

    h = jnp.concatenate([meta_tokens.astype(F32), x[0], jnp.zeros((rows_p - rows, d), F32)], axis=0)
    hb = h.astype(BF16)
    kv_shared = None

    def ffn(h, layer, half):
        return _ffn_ln(h, ffn_w_gu[layer, half].astype(BF16), ffn_w_down[layer, half].astype(BF16),
                       ln_g[layer, 2 * half], ln_b[layer, 2 * half], alpha=alpha, tm=tm, th=FFN_HIDDEN_TILE)

    for layer in range(depth):
        h, hb = ffn(h, layer, 0)
        if layer < n_a:
            w_in = gla_w_in[layer].astype(BF16)
            qkvr = _proj(hb, w_in[:, :2 * qk + 2 * vd], out_dtype=BF16, tm=tm, tn=PROJ_COL_TILE)
            w_gate = jnp.pad(w_in[:, 2 * qk + 2 * vd:], ((0, 0), (0, LANES - GLA_GATE_RANK)))
            g_low = _proj(hb, w_gate, out_dtype=F32, tm=tm, tn=LANES)
            w_g2 = jnp.pad(gla_w_g2[layer].astype(BF16), ((0, LANES - GLA_GATE_RANK), (0, 0)))
            mixed = _gla(qkvr, g_low, w_g2, gla_b_g2[layer], gla_norm_g[layer],
                         heads=GLA_HEADS, dk=dk, dv=dv, chunk=GLA_CHUNK)
            w_out = gla_w_out[layer].astype(BF16)
        else:
            jb = layer - n_a
            q = _proj(hb, sb_w_q[jb].astype(BF16), out_dtype=BF16, tm=tm, tn=PROJ_COL_TILE, scale=dh ** -0.5)
            mixed = _sb_attention(q, kv_shared, heads=SB_HEADS, blk=SB_BLOCK)
            w_out = sb_w_out[jb].astype(BF16)
        h, hb = _mix_ln(mixed, w_out, h, ln_g[layer, 1], ln_b[layer, 1], alpha=alpha, tm=tm)
        h, hb = ffn(h, layer, 1)
        if layer == n_a - 1:
            kv_shared = _proj(hb, sb_w_kv.astype(BF16), out_dtype=BF16, tm=tm, tn=PROJ_COL_TILE)
    return h[N_META:rows][None]
```

```python
import functools
import math

import numpy as np
import jax
import jax.numpy as jnp
from jax import lax
from jax.experimental import pallas as pl
from jax.experimental.pallas import tpu as pltpu

F32 = jnp.float32
BF16 = jnp.bfloat16

N_META = 16
GLA_HEADS = 4
GLA_GATE_RANK = 16
GLA_GATE_TAU = 16.0
SB_HEADS = 16
LN_EPS = 1e-5
RMS_EPS = 1e-6

LANES = 128
VMEM_LIMIT_BYTES = 56 * 1024 * 1024

ROW_TILE = 640
ROW_ALIGN = 1280
FFN_HIDDEN_TILE = 512
PROJ_COL_TILE = 512
GLA_CHUNK = 128
SB_BLOCK = 256
SB_GROUP = ROW_ALIGN // SB_BLOCK
SB_LOG_WEIGHT_FLOOR = -105.0


def _dot(a, b):
    return jnp.dot(a, b, preferred_element_type=F32)


def _dot_nt(a, b):
    return lax.dot_general(a, b, (((1,), (1,)), ((), ())), preferred_element_type=F32)


def _dot_tn(a, b):
    return lax.dot_general(a, b, (((0,), (0,)), ((), ())), preferred_element_type=F32)


def _log_sigmoid(x):
    return -(jnp.maximum(-x, 0.0) + jnp.log1p(jnp.exp(-jnp.abs(x))))


def _silu(x):
    return x * jax.nn.sigmoid(x)


def _layer_norm(y, g, b):
    mu = jnp.mean(y, -1, keepdims=True)
    yc = y - mu
    var = jnp.mean(yc * yc, -1, keepdims=True)
    return yc * lax.rsqrt(var + LN_EPS) * g + b


def _params(*sem):
    return pltpu.CompilerParams(dimension_semantics=sem, vmem_limit_bytes=VMEM_LIMIT_BYTES)


def _ffn_ln_kernel(h_ref, wg_ref, wu_ref, wd_ref, g_ref, b_ref, o_ref, ob_ref, xb_ref, acc_ref, *, alpha):
    j = pl.program_id(1)

    @pl.when(j == 0)
    def _():
        xb_ref[...] = h_ref[...].astype(BF16)
        acc_ref[...] = jnp.zeros_like(acc_ref)

    x = xb_ref[...]
    gate = _dot(x, wg_ref[...])
    up = _dot(x, wu_ref[...])
    act = (_silu(gate) * up).astype(BF16)
    acc_ref[...] += _dot(act, wd_ref[...])

    @pl.when(j == pl.num_programs(1) - 1)
    def _():
        y = _layer_norm(alpha * h_ref[...] + 0.5 * acc_ref[...], g_ref[...], b_ref[...])
        o_ref[...] = y
        ob_ref[...] = y.astype(BF16)


def _ffn_ln(h, w_gu, w_down, which, g, b, *, alpha, tm, th):
    rows, d = h.shape
    hidden = w_down.shape[2]
    nh = hidden // th
    assert rows % tm == 0 and hidden % th == 0 and w_gu.shape[2:] == (d, 2 * hidden)
    return pl.pallas_call(
        functools.partial(_ffn_ln_kernel, alpha=alpha),
        grid=(rows // tm, nh),
        in_specs=[
            pl.BlockSpec((tm, d), lambda i, j: (i, 0)),
            pl.BlockSpec((None, None, d, th), lambda i, j: (*which, 0, j)),
            pl.BlockSpec((None, None, d, th), lambda i, j: (*which, 0, j + nh)),
            pl.BlockSpec((None, None, th, d), lambda i, j: (*which, j, 0)),
            pl.BlockSpec((1, d), lambda i, j: (0, 0)),
            pl.BlockSpec((1, d), lambda i, j: (0, 0)),
        ],
        out_specs=[
            pl.BlockSpec((tm, d), lambda i, j: (i, 0)),
            pl.BlockSpec((tm, d), lambda i, j: (i, 0)),
        ],
        out_shape=[jax.ShapeDtypeStruct((rows, d), F32), jax.ShapeDtypeStruct((rows, d), BF16)],
        scratch_shapes=[pltpu.VMEM((tm, d), BF16), pltpu.VMEM((tm, d), F32)],
        compiler_params=_params("parallel", "arbitrary"),
        name="ffn_ln",
    )(h, w_gu, w_gu, w_down, g.reshape(1, d), b.reshape(1, d))


def _proj_kernel(x_ref, w_ref, o_ref, *, scale):
    y = _dot(x_ref[...], w_ref[...])
    if scale != 1.0:
        y = y * scale
    o_ref[...] = y.astype(o_ref.dtype)


def _proj(x, w, *, out_dtype, tm, tn, scale=1.0, n=None):
    rows, k = x.shape
    n = w.shape[1] if n is None else n
    assert rows % tm == 0 and n % tn == 0 and w.shape[0] == k and n <= w.shape[1]
    return pl.pallas_call(
        functools.partial(_proj_kernel, scale=scale),
        grid=(rows // tm, n // tn),
        in_specs=[
            pl.BlockSpec((tm, k), lambda i, j: (i, 0)),
            pl.BlockSpec((k, tn), lambda i, j: (0, j)),
        ],
        out_specs=pl.BlockSpec((tm, tn), lambda i, j: (i, j)),
        out_shape=jax.ShapeDtypeStruct((rows, n), out_dtype),
        compiler_params=_params("parallel", "arbitrary"),
        name="proj",
    )(x, w)


def _mix_ln_kernel(a_ref, w_ref, h_ref, g_ref, b_ref, o_ref, ob_ref, *, alpha):
    mix = _dot(a_ref[...], w_ref[...])
    y = _layer_norm(alpha * h_ref[...] + mix, g_ref[...], b_ref[...])
    o_ref[...] = y
    ob_ref[...] = y.astype(BF16)


def _mix_ln(a, w, h, g, b, *, alpha, tm):
    rows, d = h.shape
    k = a.shape[1]
    assert rows % tm == 0 and w.shape == (k, d)
    return pl.pallas_call(
        functools.partial(_mix_ln_kernel, alpha=alpha),
        grid=(rows // tm,),
        in_specs=[
            pl.BlockSpec((tm, k), lambda i: (i, 0)),
            pl.BlockSpec((k, d), lambda i: (0, 0)),
            pl.BlockSpec((tm, d), lambda i: (i, 0)),
            pl.BlockSpec((1, d), lambda i: (0, 0)),
            pl.BlockSpec((1, d), lambda i: (0, 0)),
        ],
        out_specs=[
            pl.BlockSpec((tm, d), lambda i: (i, 0)),
            pl.BlockSpec((tm, d), lambda i: (i, 0)),
        ],
        out_shape=[jax.ShapeDtypeStruct((rows, d), F32), jax.ShapeDtypeStruct((rows, d), BF16)],
        compiler_params=_params("parallel"),
        name="mix_ln",
    )(a, w, h, g.reshape(1, d), b.reshape(1, d))


def _gla_tables(c):
    n_lev = int(math.log2(c))
    assert 1 << n_lev == c
    t = np.arange(c)[:, None]
    j = np.arange(c)[None, :]
    blocks, masks = [], [np.eye(c, dtype=np.float32)]
    for lev in range(n_lev):
        m = c >> (lev + 1)
        mid = (t // (2 * m)) * (2 * m) + m
        upper = t >= mid
        a = np.where(upper, (j >= mid) & (j <= t), (j > t) & (j <= mid - 1))
        blocks.append(a)
        s_mid = (j // (2 * m)) * (2 * m) + m
        masks.append(((t // (2 * m)) == (j // (2 * m))) & upper & (j < s_mid))
    blocks.append(j <= t)
    blocks.append(j > t)
    sums = np.concatenate(blocks, axis=0).astype(np.float32)
    sums = np.concatenate([sums, sums], axis=1)
    return n_lev, jnp.asarray(sums, BF16), jnp.asarray(np.stack(masks).astype(np.float32), F32)


def _gla_kernel(q_ref, k_ref, v_ref, r_ref, gl_ref, wg2_ref, bg2_ref, ng_ref, sums_ref, masks_ref,
                o_ref, st_ref, *, heads, dk, dv, n_lev, q_scale):
    c = q_ref.shape[0]

    @pl.when(pl.program_id(0) == 0)
    def _():
        st_ref[...] = jnp.zeros_like(st_ref)

    g_low = gl_ref[...].astype(BF16)
    for hd in range(heads):
        ks = slice(hd * dk, (hd + 1) * dk)
        vs = slice(hd * dv, (hd + 1) * dv)
        x = _dot(g_low, wg2_ref[:, ks]) + bg2_ref[:, ks]
        lg = _log_sigmoid(x) * (1.0 / GLA_GATE_TAU)
        lg_hi = lg.astype(BF16)
        lg_lo = (lg - lg_hi.astype(F32)).astype(BF16)
        fac = jnp.exp(_dot(sums_ref[...], jnp.concatenate([lg_hi, lg_lo], axis=0)))
        q = q_ref[:, ks].astype(F32) * q_scale
        k = k_ref[:, ks].astype(F32)
        v = v_ref[:, vs]
        scores = masks_ref[0] * _dot_nt(q.astype(BF16), k.astype(BF16))
        for lev in range(n_lev):
            f = fac[lev * c:(lev + 1) * c]
            scores = scores + masks_ref[lev + 1] * _dot_nt((q * f).astype(BF16), (k * f).astype(BF16))
        f_cum = fac[n_lev * c:(n_lev + 1) * c]
        f_dec = fac[(n_lev + 1) * c:(n_lev + 2) * c]
        state_t = st_ref[hd]
        o = _dot(scores.astype(BF16), v) + _dot_nt((q * f_cum).astype(BF16), state_t.astype(BF16))
        st_ref[hd] = state_t * f_cum[c - 1:c, :] + _dot_tn(v, (k * f_dec).astype(BF16))
        o = o * lax.rsqrt(jnp.mean(o * o, -1, keepdims=True) + RMS_EPS) * ng_ref[...]
        r = r_ref[:, vs].astype(F32)
        o_ref[:, vs] = (o * _silu(r)).astype(o_ref.dtype)


def _gla(qkvr, g_low, w_g2, b_g2, norm_g, *, heads, dk, dv, chunk):
    rows = qkvr.shape[0]
    qk, vd = heads * dk, heads * dv
    assert rows % chunk == 0 and qkvr.shape[1] == 2 * qk + 2 * vd and vd == 2 * qk
    n_lev, sums, masks = _gla_tables(chunk)
    rank_pad = g_low.shape[1]
    return pl.pallas_call(
        functools.partial(_gla_kernel, heads=heads, dk=dk, dv=dv, n_lev=n_lev, q_scale=dk ** -0.5),
        grid=(rows // chunk,),
        in_specs=[
            pl.BlockSpec((chunk, qk), lambda i: (i, 0)),
            pl.BlockSpec((chunk, qk), lambda i: (i, 1)),
            pl.BlockSpec((chunk, vd), lambda i: (i, 1)),
            pl.BlockSpec((chunk, vd), lambda i: (i, 2)),
            pl.BlockSpec((chunk, rank_pad), lambda i: (i, 0)),
            pl.BlockSpec((rank_pad, qk), lambda i: (0, 0)),
            pl.BlockSpec((1, qk), lambda i: (0, 0)),
            pl.BlockSpec((1, dv), lambda i: (0, 0)),
            pl.BlockSpec(sums.shape, lambda i: (0, 0)),
            pl.BlockSpec(masks.shape, lambda i: (0, 0, 0)),
        ],
        out_specs=pl.BlockSpec((chunk, vd), lambda i: (i, 0)),
        out_shape=jax.ShapeDtypeStruct((rows, vd), BF16),
        scratch_shapes=[pltpu.VMEM((heads, dv, dk), F32)],
        compiler_params=_params("arbitrary"),
        name="gla",
    )(qkvr, qkvr, qkvr, qkvr, g_low, w_g2, b_g2.reshape(1, qk), norm_g.reshape(1, dv), sums, masks)


def _sb_tables(tk):
    j = np.arange(tk)[:, None]
    s = np.arange(tk)[None, :]
    tab = np.concatenate([(j >= s), np.ones((tk, LANES), bool)], axis=1)
    return jnp.asarray(-tab.astype(np.float32), BF16)


def _sb_tile(q, k_tile, v_tile, tab, carry, acc, *, diagonal):
    blk = q.shape[0]
    z = _dot_nt(q, k_tile)
    softplus = jnp.maximum(z, 0.0) + jnp.log(1.0 + jnp.exp(-jnp.abs(z)))
    if diagonal:
        strict = (lax.broadcasted_iota(jnp.int32, (blk, blk), 1)
                  < lax.broadcasted_iota(jnp.int32, (blk, blk), 0))
        softplus = jnp.where(strict, softplus, 0.0)
    sums = _dot(softplus.astype(BF16), tab)
    logit = z + sums[:, :blk]
    if carry is not None:
        logit = logit + jnp.concatenate([carry] * (blk // LANES), axis=1)
    w = jnp.exp(logit)
    if diagonal:
        w = jnp.where(strict, w, 0.0)
    out = _dot(w.astype(BF16), v_tile)
    total = sums[:, blk:]
    return (total if carry is None else carry + total), (out if acc is None else acc + out)


def _sb_kernel(q_ref, k_ref, v_ref, tab_ref, o_ref, acc_ref, carry_ref, *, blk, n_sub):
    base = pl.program_id(1) * n_sub
    tab = tab_ref[...]

    def rows_of(s):
        return pl.ds(s * blk, blk)

    def keys_of(j):
        return pl.ds(pl.multiple_of(j * blk, blk), blk)

    for s in range(n_sub):
        q = q_ref[rows_of(s), :]
        carry, acc = _sb_tile(q, k_ref[keys_of(base + s), :], v_ref[keys_of(base + s), :], tab,
                              None, None, diagonal=True)
        below = jnp.maximum(base + s - 1, 0)
        carry2, acc2 = _sb_tile(q, k_ref[keys_of(below), :], v_ref[keys_of(below), :], tab,
                                carry, acc, diagonal=False)
        if s == 0:
            carry2 = jnp.where(base > 0, carry2, carry)
            acc2 = jnp.where(base > 0, acc2, acc)
        carry_ref[rows_of(s), :] = carry2
        acc_ref[rows_of(s), :] = acc2

    def walk(s, j):
        carry, acc = _sb_tile(q_ref[rows_of(s), :], k_ref[keys_of(j), :], v_ref[keys_of(j), :], tab,
                              carry_ref[rows_of(s), :], acc_ref[rows_of(s), :], diagonal=False)
        carry_ref[rows_of(s), :] = carry
        acc_ref[rows_of(s), :] = acc

    tops = [jnp.max(carry_ref[rows_of(s), :]) for s in range(n_sub)]
    for s in range(n_sub):
        def cond(state):
            j, top = state
            return jnp.logical_and(j >= 0, top > SB_LOG_WEIGHT_FLOOR)

        def body(state, s=s):
            j, _ = state
            walk(s, j)
            return j - 1, jnp.max(carry_ref[rows_of(s), :])

        lax.while_loop(cond, body, (base + s - 2, tops[s]))
    o_ref[...] = acc_ref[...].astype(o_ref.dtype)


def _sb_attention(q, kv, *, heads, blk, n_sub):
    rows, d = q.shape
    dh = d // heads
    step = blk * n_sub
    assert rows % step == 0 and dh % LANES == 0 and blk % LANES == 0 and kv.shape == (rows, 2 * d)
    tab = _sb_tables(blk)
    return pl.pallas_call(
        functools.partial(_sb_kernel, blk=blk, n_sub=n_sub),
        grid=(heads, rows // step),
        in_specs=[
            pl.BlockSpec((step, dh), lambda h, i: (i, h)),
            pl.BlockSpec((rows, dh), lambda h, i: (0, h)),
            pl.BlockSpec((rows, dh), lambda h, i: (0, h + heads)),
            pl.BlockSpec(tab.shape, lambda h, i: (0, 0)),
        ],
        out_specs=pl.BlockSpec((step, dh), lambda h, i: (i, h)),
        out_shape=jax.ShapeDtypeStruct((rows, d), BF16),
        scratch_shapes=[pltpu.VMEM((step, dh), F32), pltpu.VMEM((step, LANES), F32)],
        compiler_params=_params("parallel", "arbitrary"),
        name="sb_attention",
    )(q, kv, kv, tab)


def _tiles(rows_padded):
    tm = ROW_TILE if rows_padded % ROW_TILE == 0 else SB_BLOCK
    return dict(tm=tm)


def kernel(x, meta_tokens, ffn_w_gu, ffn_w_down, ln_g, ln_b, gla_w_in, gla_w_g2, gla_b_g2, gla_norm_g,
           gla_w_out, sb_w_kv, sb_w_q, sb_w_out):
    batch, seq, d = x.shape
    assert batch == 1 and meta_tokens.shape == (N_META, d)
    depth = ffn_w_gu.shape[0]
    n_a = gla_w_in.shape[0]
    alpha = (2.0 * depth) ** 0.25
    rows = N_META + seq
    rows_p = -(-rows // ROW_ALIGN) * ROW_ALIGN
    tm = _tiles(rows_p)["tm"]

    qk = gla_w_g2.shape[2]
    vd = gla_w_out.shape[1]
    dk, dv = qk // GLA_HEADS, vd // GLA_HEADS
    dh = d // SB_HEADS

    h = jnp.concatenate([meta_tokens.astype(F32), x[0], jnp.zeros((rows_p - rows, d), F32)], axis=0)
    hb = h.astype(BF16)
    kv_shared = None

    w_gu_all = ffn_w_gu.astype(BF16)
    w_down_all = ffn_w_down.astype(BF16)

    def ffn(h, layer, half):
        return _ffn_ln(h, w_gu_all, w_down_all, (layer, half), ln_g[layer, 2 * half], ln_b[layer, 2 * half],
                       alpha=alpha, tm=tm, th=FFN_HIDDEN_TILE)

    for layer in range(depth):
        h, hb = ffn(h, layer, 0)
        if layer < n_a:
            w_in = gla_w_in[layer].astype(BF16)
            qkvr = _proj(hb, w_in, out_dtype=BF16, tm=tm, tn=PROJ_COL_TILE, n=2 * qk + 2 * vd)
            w_gate = jnp.pad(w_in[:, 2 * qk + 2 * vd:], ((0, 0), (0, LANES - GLA_GATE_RANK)))
            g_low = _proj(hb, w_gate, out_dtype=F32, tm=tm, tn=LANES)
            w_g2 = jnp.pad(gla_w_g2[layer].astype(BF16), ((0, LANES - GLA_GATE_RANK), (0, 0)))
            mixed = _gla(qkvr, g_low, w_g2, gla_b_g2[layer], gla_norm_g[layer],
                         heads=GLA_HEADS, dk=dk, dv=dv, chunk=GLA_CHUNK)
            w_out = gla_w_out[layer].astype(BF16)
        else:
            jb = layer - n_a
            q = _proj(hb, sb_w_q[jb].astype(BF16), out_dtype=BF16, tm=tm, tn=PROJ_COL_TILE, scale=dh ** -0.5)
            mixed = _sb_attention(q, kv_shared, heads=SB_HEADS, blk=SB_BLOCK, n_sub=SB_GROUP)
            w_out = sb_w_out[jb].astype(BF16)
        h, hb = _mix_ln(mixed, w_out, h, ln_g[layer, 1], ln_b[layer, 1], alpha=alpha, tm=tm)
        h, hb = ffn(h, layer, 1)
        if layer == n_a - 1:
            kv_shared = _proj(hb, sb_w_kv.astype(BF16), out_dtype=BF16, tm=tm, tn=PROJ_COL_TILE)
    return h[N_META:rows][None]
```

```python
import functools
import math

import numpy as np
import jax
import jax.numpy as jnp
from jax import lax
from jax.experimental import pallas as pl
from jax.experimental.pallas import tpu as pltpu

F32 = jnp.float32
BF16 = jnp.bfloat16

N_META = 16
GLA_HEADS = 4
GLA_GATE_RANK = 16
GLA_GATE_TAU = 16.0
SB_HEADS = 16
LN_EPS = 1e-5
RMS_EPS = 1e-6

LANES = 128
VMEM_LIMIT_BYTES = 56 * 1024 * 1024

ROW_TILE = 640
ROW_ALIGN = 1280
FFN_HIDDEN_TILE = 512
PROJ_COL_TILE = 2048
EPILOGUE_SPLIT = 4
GLA_CHUNK = 128
SB_BLOCK = 256
SB_GROUP = ROW_ALIGN // SB_BLOCK
SB_LOG_WEIGHT_FLOOR = -105.0


def _dot(a, b):
    return jnp.dot(a, b, preferred_element_type=F32)


def _dot_nt(a, b):
    return lax.dot_general(a, b, (((1,), (1,)), ((), ())), preferred_element_type=F32)


def _dot_tn(a, b):
    return lax.dot_general(a, b, (((0,), (0,)), ((), ())), preferred_element_type=F32)


def _log_sigmoid(x):
    return -(jnp.maximum(-x, 0.0) + jnp.log1p(jnp.exp(-jnp.abs(x))))


def _silu(x):
    return x * jax.nn.sigmoid(x)


def _layer_norm(y, g, b):
    mu = jnp.mean(y, -1, keepdims=True)
    yc = y - mu
    var = jnp.mean(yc * yc, -1, keepdims=True)
    return yc * lax.rsqrt(var + LN_EPS) * g + b


def _params(*sem):
    return pltpu.CompilerParams(dimension_semantics=sem, vmem_limit_bytes=VMEM_LIMIT_BYTES)


def _ffn_ln_kernel(h_ref, wg_ref, wu_ref, wd_ref, g_ref, b_ref, o_ref, ob_ref, xb_ref, acc_ref, *, alpha):
    j = pl.program_id(1)

    @pl.when(j == 0)
    def _():
        xb_ref[...] = h_ref[...].astype(BF16)
        acc_ref[...] = jnp.zeros_like(acc_ref)

    x = xb_ref[...]
    gate = _dot(x, wg_ref[...])
    up = _dot(x, wu_ref[...])
    act = (_silu(gate) * up).astype(BF16)
    last = pl.num_programs(1) - 1

    @pl.when(j < last)
    def _():
        acc_ref[...] += _dot(act, wd_ref[...])

    @pl.when(j == last)
    def _():
        part = act.shape[0] // EPILOGUE_SPLIT
        for r in range(EPILOGUE_SPLIT):
            rows = slice(r * part, (r + 1) * part)
            ffn = acc_ref[rows, :] + _dot(act[rows], wd_ref[...])
            y = _layer_norm(alpha * h_ref[rows, :] + 0.5 * ffn, g_ref[...], b_ref[...])
            o_ref[rows, :] = y
            ob_ref[rows, :] = y.astype(BF16)


def _ffn_ln(h, w_gu, w_down, which, g, b, *, alpha, tm, th):
    rows, d = h.shape
    hidden = w_down.shape[2]
    nh = hidden // th
    assert rows % tm == 0 and hidden % th == 0 and w_gu.shape[2:] == (d, 2 * hidden)
    return pl.pallas_call(
        functools.partial(_ffn_ln_kernel, alpha=alpha),
        grid=(rows // tm, nh),
        in_specs=[
            pl.BlockSpec((tm, d), lambda i, j: (i, 0)),
            pl.BlockSpec((None, None, d, th), lambda i, j: (*which, 0, j)),
            pl.BlockSpec((None, None, d, th), lambda i, j: (*which, 0, j + nh)),
            pl.BlockSpec((None, None, th, d), lambda i, j: (*which, j, 0)),
            pl.BlockSpec((1, d), lambda i, j: (0, 0)),
            pl.BlockSpec((1, d), lambda i, j: (0, 0)),
        ],
        out_specs=[
            pl.BlockSpec((tm, d), lambda i, j: (i, 0)),
            pl.BlockSpec((tm, d), lambda i, j: (i, 0)),
        ],
        out_shape=[jax.ShapeDtypeStruct((rows, d), F32), jax.ShapeDtypeStruct((rows, d), BF16)],
        scratch_shapes=[pltpu.VMEM((tm, d), BF16), pltpu.VMEM((tm, d), F32)],
        compiler_params=_params("parallel", "arbitrary"),
        name="ffn_ln",
    )(h, w_gu, w_gu, w_down, g.reshape(1, d), b.reshape(1, d))


def _proj_kernel(x_ref, w_ref, o_ref, *, scale):
    y = _dot(x_ref[...], w_ref[...])
    if scale != 1.0:
        y = y * scale
    o_ref[...] = y.astype(o_ref.dtype)


def _proj(x, w, *, out_dtype, tm, tn, scale=1.0, n=None):
    rows, k = x.shape
    n = w.shape[1] if n is None else n
    assert rows % tm == 0 and n % tn == 0 and w.shape[0] == k and n <= w.shape[1]
    tn = min(tn, n)
    return pl.pallas_call(
        functools.partial(_proj_kernel, scale=scale),
        grid=(n // tn, rows // tm),
        in_specs=[
            pl.BlockSpec((tm, k), lambda j, i: (i, 0)),
            pl.BlockSpec((k, tn), lambda j, i: (0, j)),
        ],
        out_specs=pl.BlockSpec((tm, tn), lambda j, i: (i, j)),
        out_shape=jax.ShapeDtypeStruct((rows, n), out_dtype),
        compiler_params=_params("parallel", "arbitrary"),
        name="proj",
    )(x, w)


def _mix_ln_kernel(a_ref, w_ref, h_ref, g_ref, b_ref, o_ref, ob_ref, *, alpha):
    part = a_ref.shape[0] // EPILOGUE_SPLIT
    for r in range(EPILOGUE_SPLIT):
        rows = slice(r * part, (r + 1) * part)
        mix = _dot(a_ref[rows, :], w_ref[...])
        y = _layer_norm(alpha * h_ref[rows, :] + mix, g_ref[...], b_ref[...])
        o_ref[rows, :] = y
        ob_ref[rows, :] = y.astype(BF16)


def _mix_ln(a, w, h, g, b, *, alpha, tm):
    rows, d = h.shape
    k = a.shape[1]
    assert rows % tm == 0 and w.shape == (k, d)
    return pl.pallas_call(
        functools.partial(_mix_ln_kernel, alpha=alpha),
        grid=(rows // tm,),
        in_specs=[
            pl.BlockSpec((tm, k), lambda i: (i, 0)),
            pl.BlockSpec((k, d), lambda i: (0, 0)),
            pl.BlockSpec((tm, d), lambda i: (i, 0)),
            pl.BlockSpec((1, d), lambda i: (0, 0)),
            pl.BlockSpec((1, d), lambda i: (0, 0)),
        ],
        out_specs=[
            pl.BlockSpec((tm, d), lambda i: (i, 0)),
            pl.BlockSpec((tm, d), lambda i: (i, 0)),
        ],
        out_shape=[jax.ShapeDtypeStruct((rows, d), F32), jax.ShapeDtypeStruct((rows, d), BF16)],
        compiler_params=_params("parallel"),
        name="mix_ln",
    )(a, w, h, g.reshape(1, d), b.reshape(1, d))


def _gla_tables(c):
    n_lev = int(math.log2(c))
    assert 1 << n_lev == c
    t = np.arange(c)[:, None]
    j = np.arange(c)[None, :]
    blocks, masks = [], [np.eye(c, dtype=np.float32)]
    for lev in range(n_lev):
        m = c >> (lev + 1)
        mid = (t // (2 * m)) * (2 * m) + m
        upper = t >= mid
        a = np.where(upper, (j >= mid) & (j <= t), (j > t) & (j <= mid - 1))
        blocks.append(a)
        s_mid = (j // (2 * m)) * (2 * m) + m
        masks.append(((t // (2 * m)) == (j // (2 * m))) & upper & (j < s_mid))
    blocks.append(j <= t)
    blocks.append(j > t)
    sums = np.concatenate(blocks, axis=0).astype(np.float32)
    sums = np.concatenate([sums, sums], axis=1)
    return n_lev, jnp.asarray(sums, BF16), jnp.asarray(np.stack(masks).astype(np.float32), F32)


def _gla_kernel(q_ref, k_ref, v_ref, r_ref, gl_ref, wg2_ref, bg2_ref, ng_ref, sums_ref, masks_ref,
                o_ref, st_ref, *, heads, dk, dv, n_lev, q_scale):
    c = q_ref.shape[0]

    @pl.when(pl.program_id(0) == 0)
    def _():
        st_ref[...] = jnp.zeros_like(st_ref)

    g_low = gl_ref[...].astype(BF16)
    for hd in range(heads):
        ks = slice(hd * dk, (hd + 1) * dk)
        vs = slice(hd * dv, (hd + 1) * dv)
        x = _dot(g_low, wg2_ref[:, ks]) + bg2_ref[:, ks]
        lg = _log_sigmoid(x) * (math.log2(math.e) / GLA_GATE_TAU)
        lg_hi = lg.astype(BF16)
        lg_lo = (lg - lg_hi.astype(F32)).astype(BF16)
        fac = jnp.exp2(_dot(sums_ref[...], jnp.concatenate([lg_hi, lg_lo], axis=0)))
        q = q_ref[:, ks].astype(F32) * q_scale
        k = k_ref[:, ks].astype(F32)
        v = v_ref[:, vs]
        scores = masks_ref[0] * _dot_nt(q.astype(BF16), k.astype(BF16))
        for lev in range(n_lev):
            f = fac[lev * c:(lev + 1) * c]
            scores = scores + masks_ref[lev + 1] * _dot_nt((q * f).astype(BF16), (k * f).astype(BF16))
        f_cum = fac[n_lev * c:(n_lev + 1) * c]
        f_dec = fac[(n_lev + 1) * c:(n_lev + 2) * c]
        state_t = st_ref[hd]
        o = _dot(scores.astype(BF16), v) + _dot_nt((q * f_cum).astype(BF16), state_t.astype(BF16))
        st_ref[hd] = state_t * f_cum[c - 1:c, :] + _dot_tn(v, (k * f_dec).astype(BF16))
        o = o * lax.rsqrt(jnp.mean(o * o, -1, keepdims=True) + RMS_EPS) * ng_ref[...]
        r = r_ref[:, vs].astype(F32)
        o_ref[:, vs] = (o * _silu(r)).astype(o_ref.dtype)


def _gla(qkvr, g_low, w_g2, b_g2, norm_g, *, heads, dk, dv, chunk):
    rows = qkvr.shape[0]
    qk, vd = heads * dk, heads * dv
    assert rows % chunk == 0 and qkvr.shape[1] == 2 * qk + 2 * vd and vd == 2 * qk
    n_lev, sums, masks = _gla_tables(chunk)
    rank_pad = g_low.shape[1]
    return pl.pallas_call(
        functools.partial(_gla_kernel, heads=heads, dk=dk, dv=dv, n_lev=n_lev, q_scale=dk ** -0.5),
        grid=(rows // chunk,),
        in_specs=[
            pl.BlockSpec((chunk, qk), lambda i: (i, 0)),
            pl.BlockSpec((chunk, qk), lambda i: (i, 1)),
            pl.BlockSpec((chunk, vd), lambda i: (i, 1)),
            pl.BlockSpec((chunk, vd), lambda i: (i, 2)),
            pl.BlockSpec((chunk, rank_pad), lambda i: (i, 0)),
            pl.BlockSpec((rank_pad, qk), lambda i: (0, 0)),
            pl.BlockSpec((1, qk), lambda i: (0, 0)),
            pl.BlockSpec((1, dv), lambda i: (0, 0)),
            pl.BlockSpec(sums.shape, lambda i: (0, 0)),
            pl.BlockSpec(masks.shape, lambda i: (0, 0, 0)),
        ],
        out_specs=pl.BlockSpec((chunk, vd), lambda i: (i, 0)),
        out_shape=jax.ShapeDtypeStruct((rows, vd), BF16),
        scratch_shapes=[pltpu.VMEM((heads, dv, dk), F32)],
        compiler_params=_params("arbitrary"),
        name="gla",
    )(qkvr, qkvr, qkvr, qkvr, g_low, w_g2, b_g2.reshape(1, qk), norm_g.reshape(1, dv), sums, masks)


def _sb_tables(tk):
    j = np.arange(tk)[:, None]
    s = np.arange(tk)[None, :]
    tab = np.concatenate([(j >= s), np.ones((tk, LANES), bool)], axis=1)
    return jnp.asarray(-tab.astype(np.float32), BF16)


def _sb_tile(q, k_tile, v_tile, tab, carry, acc, *, diagonal):
    blk = q.shape[0]
    z = _dot_nt(q, k_tile)
    softplus = jnp.maximum(z, 0.0) + jnp.log(1.0 + jnp.exp(-jnp.abs(z)))
    if diagonal:
        strict = (lax.broadcasted_iota(jnp.int32, (blk, blk), 1)
                  < lax.broadcasted_iota(jnp.int32, (blk, blk), 0))
        softplus = jnp.where(strict, softplus, 0.0)
    sums = _dot(softplus.astype(BF16), tab)
    logit = z + sums[:, :blk]
    if carry is not None:
        logit = logit + jnp.concatenate([carry] * (blk // LANES), axis=1)
    w = jnp.exp(logit)
    if diagonal:
        w = jnp.where(strict, w, 0.0)
    out = _dot(w.astype(BF16), v_tile)
    total = sums[:, blk:]
    return (total if carry is None else carry + total), (out if acc is None else acc + out)


def _sb_kernel(q_ref, k_ref, v_ref, tab_ref, o_ref, acc_ref, carry_ref, *, blk, n_sub):
    base = pl.program_id(1) * n_sub
    tab = tab_ref[...]

    def rows_of(s):
        return pl.ds(s * blk, blk)

    def keys_of(j):
        return pl.ds(pl.multiple_of(j * blk, blk), blk)

    for s in range(n_sub):
        q = q_ref[rows_of(s), :]
        carry, acc = _sb_tile(q, k_ref[keys_of(base + s), :], v_ref[keys_of(base + s), :], tab,
                              None, None, diagonal=True)
        below = jnp.maximum(base + s - 1, 0)
        carry2, acc2 = _sb_tile(q, k_ref[keys_of(below), :], v_ref[keys_of(below), :], tab,
                                carry, acc, diagonal=False)
        if s == 0:
            carry2 = jnp.where(base > 0, carry2, carry)
            acc2 = jnp.where(base > 0, acc2, acc)
        carry_ref[rows_of(s), :] = carry2
        acc_ref[rows_of(s), :] = acc2

    def walk(s, j):
        carry, acc = _sb_tile(q_ref[rows_of(s), :], k_ref[keys_of(j), :], v_ref[keys_of(j), :], tab,
                              carry_ref[rows_of(s), :], acc_ref[rows_of(s), :], diagonal=False)
        carry_ref[rows_of(s), :] = carry
        acc_ref[rows_of(s), :] = acc

    tops = [jnp.max(carry_ref[rows_of(s), :]) for s in range(n_sub)]
    for s in range(n_sub):
        def cond(state):
            j, top = state
            return jnp.logical_and(j >= 0, top > SB_LOG_WEIGHT_FLOOR)

        def body(state, s=s):
            j, _ = state
            walk(s, j)
            return j - 1, jnp.max(carry_ref[rows_of(s), :])

        lax.while_loop(cond, body, (base + s - 2, tops[s]))
    o_ref[...] = acc_ref[...].astype(o_ref.dtype)


def _sb_attention(q, kv, *, heads, blk, n_sub):
    rows, d = q.shape
    dh = d // heads
    step = blk * n_sub
    assert rows % step == 0 and dh % LANES == 0 and blk % LANES == 0 and kv.shape == (rows, 2 * d)
    tab = _sb_tables(blk)
    return pl.pallas_call(
        functools.partial(_sb_kernel, blk=blk, n_sub=n_sub),
        grid=(heads, rows // step),
        in_specs=[
            pl.BlockSpec((step, dh), lambda h, i: (i, h)),
            pl.BlockSpec((rows, dh), lambda h, i: (0, h)),
            pl.BlockSpec((rows, dh), lambda h, i: (0, h + heads)),
            pl.BlockSpec(tab.shape, lambda h, i: (0, 0)),
        ],
        out_specs=pl.BlockSpec((step, dh), lambda h, i: (i, h)),
        out_shape=jax.ShapeDtypeStruct((rows, d), BF16),
        scratch_shapes=[pltpu.VMEM((step, dh), F32), pltpu.VMEM((step, LANES), F32)],
        compiler_params=_params("parallel", "arbitrary"),
        name="sb_attention",
    )(q, kv, kv, tab)


def _tiles(rows_padded):
    tm = ROW_TILE if rows_padded % ROW_TILE == 0 else SB_BLOCK
    return dict(tm=tm)


def kernel(x, meta_tokens, ffn_w_gu, ffn_w_down, ln_g, ln_b, gla_w_in, gla_w_g2, gla_b_g2, gla_norm_g,
           gla_w_out, sb_w_kv, sb_w_q, sb_w_out):
    batch, seq, d = x.shape
    assert batch == 1 and meta_tokens.shape == (N_META, d)
    depth = ffn_w_gu.shape[0]
    n_a = gla_w_in.shape[0]
    alpha = (2.0 * depth) ** 0.25
    rows = N_META + seq
    rows_p = -(-rows // ROW_ALIGN) * ROW_ALIGN
    tm = _tiles(rows_p)["tm"]

    qk = gla_w_g2.shape[2]
    vd = gla_w_out.shape[1]
    dk, dv = qk // GLA_HEADS, vd // GLA_HEADS
    dh = d // SB_HEADS

    h = jnp.concatenate([meta_tokens.astype(F32), x[0], jnp.zeros((rows_p - rows, d), F32)], axis=0)
    hb = h.astype(BF16)
    kv_shared = None

    w_gu_all = ffn_w_gu.astype(BF16)
    w_down_all = ffn_w_down.astype(BF16)

    def ffn(h, layer, half):
        return _ffn_ln(h, w_gu_all, w_down_all, (layer, half), ln_g[layer, 2 * half], ln_b[layer, 2 * half],
                       alpha=alpha, tm=tm, th=FFN_HIDDEN_TILE)

    for layer in range(depth):
        h, hb = ffn(h, layer, 0)
        if layer < n_a:
            w_in = gla_w_in[layer].astype(BF16)
            qkvr = _proj(hb, w_in, out_dtype=BF16, tm=tm, tn=PROJ_COL_TILE, n=2 * qk + 2 * vd)
            w_gate = jnp.pad(w_in[:, 2 * qk + 2 * vd:], ((0, 0), (0, LANES - GLA_GATE_RANK)))
            g_low = _proj(hb, w_gate, out_dtype=F32, tm=tm, tn=LANES)
            w_g2 = jnp.pad(gla_w_g2[layer].astype(BF16), ((0, LANES - GLA_GATE_RANK), (0, 0)))
            mixed = _gla(qkvr, g_low, w_g2, gla_b_g2[layer], gla_norm_g[layer],
                         heads=GLA_HEADS, dk=dk, dv=dv, chunk=GLA_CHUNK)
            w_out = gla_w_out[layer].astype(BF16)
        else:
            jb = layer - n_a
            q = _proj(hb, sb_w_q[jb].astype(BF16), out_dtype=BF16, tm=tm, tn=PROJ_COL_TILE, scale=dh ** -0.5)
            mixed = _sb_attention(q, kv_shared, heads=SB_HEADS, blk=SB_BLOCK, n_sub=SB_GROUP)
            w_out = sb_w_out[jb].astype(BF16)
        h, hb = _mix_ln(mixed, w_out, h, ln_g[layer, 1], ln_b[layer, 1], alpha=alpha, tm=tm)
        h, hb = ffn(h, layer, 1)
        if layer == n_a - 1:
            kv_shared = _proj(hb, sb_w_kv.astype(BF16), out_dtype=BF16, tm=tm, tn=PROJ_COL_TILE)
    return h[N_META:rows][None]
```

```python
import functools
import math

import numpy as np
import jax
import jax.numpy as jnp
from jax import lax
from jax.experimental import pallas as pl
from jax.experimental.pallas import tpu as pltpu

F32 = jnp.float32
BF16 = jnp.bfloat16

N_META = 16
GLA_HEADS = 4
GLA_GATE_RANK = 16
GLA_GATE_TAU = 16.0
SB_HEADS = 16
LN_EPS = 1e-5
RMS_EPS = 1e-6

LANES = 128
VMEM_LIMIT_BYTES = 56 * 1024 * 1024

ROW_TILE = 640
FINAL_ROW_TILE = 512
ROW_ALIGN = 1280
FFN_HIDDEN_TILE = 512
PROJ_COL_TILE = 2048
EPILOGUE_SPLIT = 4
GLA_CHUNK = 128
SB_BLOCK = 256
SB_GROUP = ROW_ALIGN // SB_BLOCK
SB_LOG_WEIGHT_FLOOR = -105.0


def _dot(a, b):
    return jnp.dot(a, b, preferred_element_type=F32)


def _dot_nt(a, b):
    return lax.dot_general(a, b, (((1,), (1,)), ((), ())), preferred_element_type=F32)


def _dot_tn(a, b):
    return lax.dot_general(a, b, (((0,), (0,)), ((), ())), preferred_element_type=F32)


def _log_sigmoid(x):
    return -(jnp.maximum(-x, 0.0) + jnp.log1p(jnp.exp(-jnp.abs(x))))


def _silu(x):
    return x * jax.nn.sigmoid(x)


def _layer_norm(y, g, b):
    mu = jnp.mean(y, -1, keepdims=True)
    yc = y - mu
    var = jnp.mean(yc * yc, -1, keepdims=True)
    return yc * lax.rsqrt(var + LN_EPS) * g + b


def _params(*sem):
    return pltpu.CompilerParams(dimension_semantics=sem, vmem_limit_bytes=VMEM_LIMIT_BYTES)


def _ffn_ln_kernel(h_ref, wg_ref, wu_ref, wd_ref, g_ref, b_ref, o_ref, ob_ref, xb_ref, acc_ref, *, alpha):
    j = pl.program_id(1)

    @pl.when(j == 0)
    def _():
        xb_ref[...] = h_ref[...].astype(BF16)
        acc_ref[...] = jnp.zeros_like(acc_ref)

    x = xb_ref[...]
    gate = _dot(x, wg_ref[...])
    up = _dot(x, wu_ref[...])
    act = (_silu(gate) * up).astype(BF16)
    acc_ref[...] += _dot(act, wd_ref[...])

    @pl.when(j == pl.num_programs(1) - 1)
    def _():
        y = _layer_norm(alpha * h_ref[...] + 0.5 * acc_ref[...], g_ref[...], b_ref[...])
        o_ref[...] = y
        ob_ref[...] = y.astype(BF16)


def _ffn_ln(h, w_gu, w_down, which, g, b, *, alpha, tm, th):
    rows, d = h.shape
    hidden = w_down.shape[2]
    nh = hidden // th
    assert rows % tm == 0 and hidden % th == 0 and w_gu.shape[2:] == (d, 2 * hidden)
    return pl.pallas_call(
        functools.partial(_ffn_ln_kernel, alpha=alpha),
        grid=(rows // tm, nh),
        in_specs=[
            pl.BlockSpec((tm, d), lambda i, j: (i, 0)),
            pl.BlockSpec((None, None, d, th), lambda i, j: (*which, 0, j)),
            pl.BlockSpec((None, None, d, th), lambda i, j: (*which, 0, j + nh)),
            pl.BlockSpec((None, None, th, d), lambda i, j: (*which, j, 0)),
            pl.BlockSpec((1, d), lambda i, j: (0, 0)),
            pl.BlockSpec((1, d), lambda i, j: (0, 0)),
        ],
        out_specs=[
            pl.BlockSpec((tm, d), lambda i, j: (i, 0)),
            pl.BlockSpec((tm, d), lambda i, j: (i, 0)),
        ],
        out_shape=[jax.ShapeDtypeStruct((rows, d), F32), jax.ShapeDtypeStruct((rows, d), BF16)],
        scratch_shapes=[pltpu.VMEM((tm, d), BF16), pltpu.VMEM((tm, d), F32)],
        compiler_params=_params("parallel", "arbitrary"),
        name="ffn_ln",
    )(h, w_gu, w_gu, w_down, g.reshape(1, d), b.reshape(1, d))


def _proj_kernel(x_ref, w_ref, o_ref, *, scale):
    y = _dot(x_ref[...], w_ref[...])
    if scale != 1.0:
        y = y * scale
    o_ref[...] = y.astype(o_ref.dtype)


def _proj(x, w, *, out_dtype, tm, tn, scale=1.0, n=None):
    rows, k = x.shape
    n = w.shape[1] if n is None else n
    assert rows % tm == 0 and n % tn == 0 and w.shape[0] == k and n <= w.shape[1]
    tn = min(tn, n)
    return pl.pallas_call(
        functools.partial(_proj_kernel, scale=scale),
        grid=(n // tn, rows // tm),
        in_specs=[
            pl.BlockSpec((tm, k), lambda j, i: (i, 0)),
            pl.BlockSpec((k, tn), lambda j, i: (0, j)),
        ],
        out_specs=pl.BlockSpec((tm, tn), lambda j, i: (i, j)),
        out_shape=jax.ShapeDtypeStruct((rows, n), out_dtype),
        compiler_params=_params("parallel", "arbitrary"),
        name="proj",
    )(x, w)


def _mix_ln_kernel(a_ref, w_ref, h_ref, g_ref, b_ref, o_ref, ob_ref, *, alpha):
    part = a_ref.shape[0] // EPILOGUE_SPLIT
    for r in range(EPILOGUE_SPLIT):
        rows = slice(r * part, (r + 1) * part)
        mix = _dot(a_ref[rows, :], w_ref[...])
        y = _layer_norm(alpha * h_ref[rows, :] + mix, g_ref[...], b_ref[...])
        o_ref[rows, :] = y
        ob_ref[rows, :] = y.astype(BF16)


def _mix_ln(a, w, h, g, b, *, alpha, tm, first_row=0, rows=None):
    d = h.shape[1]
    rows = h.shape[0] if rows is None else rows
    k = a.shape[1]
    assert rows % tm == 0 and w.shape == (k, d) and first_row + rows <= h.shape[0] and a.shape[0] == h.shape[0]
    if first_row == 0:
        a_spec = pl.BlockSpec((tm, k), lambda i: (i, 0))
        h_spec = pl.BlockSpec((tm, d), lambda i: (i, 0))
    else:
        align = math.gcd(tm, first_row)

        def window(i):
            return pl.multiple_of(i * tm + first_row, align), 0

        a_spec = pl.BlockSpec((pl.Element(tm), pl.Element(k)), window)
        h_spec = pl.BlockSpec((pl.Element(tm), pl.Element(d)), window)
    return pl.pallas_call(
        functools.partial(_mix_ln_kernel, alpha=alpha),
        grid=(rows // tm,),
        in_specs=[
            a_spec,
            pl.BlockSpec((k, d), lambda i: (0, 0)),
            h_spec,
            pl.BlockSpec((1, d), lambda i: (0, 0)),
            pl.BlockSpec((1, d), lambda i: (0, 0)),
        ],
        out_specs=[
            pl.BlockSpec((tm, d), lambda i: (i, 0)),
            pl.BlockSpec((tm, d), lambda i: (i, 0)),
        ],
        out_shape=[jax.ShapeDtypeStruct((rows, d), F32), jax.ShapeDtypeStruct((rows, d), BF16)],
        compiler_params=_params("parallel"),
        name="mix_ln",
    )(a, w, h, g.reshape(1, d), b.reshape(1, d))


def _gla_tables(c):
    n_lev = int(math.log2(c))
    assert 1 << n_lev == c
    t = np.arange(c)[:, None]
    j = np.arange(c)[None, :]
    blocks, masks = [], [np.eye(c, dtype=np.float32)]
    for lev in range(n_lev):
        m = c >> (lev + 1)
        mid = (t // (2 * m)) * (2 * m) + m
        upper = t >= mid
        a = np.where(upper, (j >= mid) & (j <= t), (j > t) & (j <= mid - 1))
        blocks.append(a)
        s_mid = (j // (2 * m)) * (2 * m) + m
        masks.append(((t // (2 * m)) == (j // (2 * m))) & upper & (j < s_mid))
    blocks.append(j <= t)
    blocks.append(j > t)
    sums = np.concatenate(blocks, axis=0).astype(np.float32)
    sums = np.concatenate([sums, sums], axis=1)
    return n_lev, jnp.asarray(sums, BF16), jnp.asarray(np.stack(masks).astype(np.float32), F32)


def _gla_kernel(q_ref, k_ref, v_ref, r_ref, gl_ref, wg2_ref, bg2_ref, ng_ref, sums_ref, masks_ref,
                o_ref, st_ref, *, heads, dk, dv, n_lev, q_scale):
    c = q_ref.shape[0]

    @pl.when(pl.program_id(0) == 0)
    def _():
        st_ref[...] = jnp.zeros_like(st_ref)

    g_low = gl_ref[...].astype(BF16)
    for hd in range(heads):
        ks = slice(hd * dk, (hd + 1) * dk)
        vs = slice(hd * dv, (hd + 1) * dv)
        x = _dot(g_low, wg2_ref[:, ks]) + bg2_ref[:, ks]
        lg = _log_sigmoid(x) * (math.log2(math.e) / GLA_GATE_TAU)
        lg_hi = lg.astype(BF16)
        lg_lo = (lg - lg_hi.astype(F32)).astype(BF16)
        fac = jnp.exp2(_dot(sums_ref[...], jnp.concatenate([lg_hi, lg_lo], axis=0)))
        q = q_ref[:, ks].astype(F32) * q_scale
        k = k_ref[:, ks].astype(F32)
        v = v_ref[:, vs]
        scores = masks_ref[0] * _dot_nt(q.astype(BF16), k.astype(BF16))
        for lev in range(n_lev):
            f = fac[lev * c:(lev + 1) * c]
            scores = scores + masks_ref[lev + 1] * _dot_nt((q * f).astype(BF16), (k * f).astype(BF16))
        f_cum = fac[n_lev * c:(n_lev + 1) * c]
        f_dec = fac[(n_lev + 1) * c:(n_lev + 2) * c]
        state_t = st_ref[hd]
        o = _dot(scores.astype(BF16), v) + _dot_nt((q * f_cum).astype(BF16), state_t.astype(BF16))
        st_ref[hd] = state_t * f_cum[c - 1:c, :] + _dot_tn(v, (k * f_dec).astype(BF16))
        o = o * lax.rsqrt(jnp.mean(o * o, -1, keepdims=True) + RMS_EPS) * ng_ref[...]
        r = r_ref[:, vs].astype(F32)
        o_ref[:, vs] = (o * _silu(r)).astype(o_ref.dtype)


def _gla(qkvr, g_low, w_g2, b_g2, norm_g, *, heads, dk, dv, chunk):
    rows = qkvr.shape[0]
    qk, vd = heads * dk, heads * dv
    assert rows % chunk == 0 and qkvr.shape[1] == 2 * qk + 2 * vd and vd == 2 * qk
    n_lev, sums, masks = _gla_tables(chunk)
    rank_pad = g_low.shape[1]
    return pl.pallas_call(
        functools.partial(_gla_kernel, heads=heads, dk=dk, dv=dv, n_lev=n_lev, q_scale=dk ** -0.5),
        grid=(rows // chunk,),
        in_specs=[
            pl.BlockSpec((chunk, qk), lambda i: (i, 0)),
            pl.BlockSpec((chunk, qk), lambda i: (i, 1)),
            pl.BlockSpec((chunk, vd), lambda i: (i, 1)),
            pl.BlockSpec((chunk, vd), lambda i: (i, 2)),
            pl.BlockSpec((chunk, rank_pad), lambda i: (i, 0)),
            pl.BlockSpec((rank_pad, qk), lambda i: (0, 0)),
            pl.BlockSpec((1, qk), lambda i: (0, 0)),
            pl.BlockSpec((1, dv), lambda i: (0, 0)),
            pl.BlockSpec(sums.shape, lambda i: (0, 0)),
            pl.BlockSpec(masks.shape, lambda i: (0, 0, 0)),
        ],
        out_specs=pl.BlockSpec((chunk, vd), lambda i: (i, 0)),
        out_shape=jax.ShapeDtypeStruct((rows, vd), BF16),
        scratch_shapes=[pltpu.VMEM((heads, dv, dk), F32)],
        compiler_params=_params("arbitrary"),
        name="gla",
    )(qkvr, qkvr, qkvr, qkvr, g_low, w_g2, b_g2.reshape(1, qk), norm_g.reshape(1, dv), sums, masks)


def _sb_tables(tk):
    j = np.arange(tk)[:, None]
    s = np.arange(tk)[None, :]
    tab = np.concatenate([(j >= s), np.ones((tk, LANES), bool)], axis=1)
    return jnp.asarray(-tab.astype(np.float32), BF16)


def _sb_tile(q, k_tile, v_tile, tab, carry, acc, *, diagonal):
    blk = q.shape[0]
    z = _dot_nt(q, k_tile)
    softplus = jnp.maximum(z, 0.0) + jnp.log(1.0 + jnp.exp(-jnp.abs(z)))
    if diagonal:
        strict = (lax.broadcasted_iota(jnp.int32, (blk, blk), 1)
                  < lax.broadcasted_iota(jnp.int32, (blk, blk), 0))
        softplus = jnp.where(strict, softplus, 0.0)
    sums = _dot(softplus.astype(BF16), tab)
    logit = z + sums[:, :blk]
    if carry is not None:
        logit = logit + jnp.concatenate([carry] * (blk // LANES), axis=1)
    w = jnp.exp(logit)
    if diagonal:
        w = jnp.where(strict, w, 0.0)
    out = _dot(w.astype(BF16), v_tile)
    total = sums[:, blk:]
    return (total if carry is None else carry + total), (out if acc is None else acc + out)


def _sb_kernel(q_ref, k_ref, v_ref, tab_ref, o_ref, acc_ref, carry_ref, *, blk, n_sub):
    base = pl.program_id(1) * n_sub
    tab = tab_ref[...]

    def rows_of(s):
        return pl.ds(s * blk, blk)

    def keys_of(j):
        return pl.ds(pl.multiple_of(j * blk, blk), blk)

    for s in range(n_sub):
        q = q_ref[rows_of(s), :]
        carry, acc = _sb_tile(q, k_ref[keys_of(base + s), :], v_ref[keys_of(base + s), :], tab,
                              None, None, diagonal=True)
        below = jnp.maximum(base + s - 1, 0)
        carry2, acc2 = _sb_tile(q, k_ref[keys_of(below), :], v_ref[keys_of(below), :], tab,
                                carry, acc, diagonal=False)
        if s == 0:
            carry2 = jnp.where(base > 0, carry2, carry)
            acc2 = jnp.where(base > 0, acc2, acc)
        carry_ref[rows_of(s), :] = carry2
        acc_ref[rows_of(s), :] = acc2

    def walk(s, j):
        carry, acc = _sb_tile(q_ref[rows_of(s), :], k_ref[keys_of(j), :], v_ref[keys_of(j), :], tab,
                              carry_ref[rows_of(s), :], acc_ref[rows_of(s), :], diagonal=False)
        carry_ref[rows_of(s), :] = carry
        acc_ref[rows_of(s), :] = acc

    tops = [jnp.max(carry_ref[rows_of(s), :]) for s in range(n_sub)]
    for s in range(n_sub):
        def cond(state):
            j, top = state
            return jnp.logical_and(j >= 0, top > SB_LOG_WEIGHT_FLOOR)

        def body(state, s=s):
            j, _ = state
            walk(s, j)
            return j - 1, jnp.max(carry_ref[rows_of(s), :])

        lax.while_loop(cond, body, (base + s - 2, tops[s]))
    o_ref[...] = acc_ref[...].astype(o_ref.dtype)


def _sb_attention(q, kv, *, heads, blk, n_sub):
    rows, d = q.shape
    dh = d // heads
    step = blk * n_sub
    assert rows % step == 0 and dh % LANES == 0 and blk % LANES == 0 and kv.shape == (rows, 2 * d)
    tab = _sb_tables(blk)
    return pl.pallas_call(
        functools.partial(_sb_kernel, blk=blk, n_sub=n_sub),
        grid=(heads, rows // step),
        in_specs=[
            pl.BlockSpec((step, dh), lambda h, i: (i, h)),
            pl.BlockSpec((rows, dh), lambda h, i: (0, h)),
            pl.BlockSpec((rows, dh), lambda h, i: (0, h + heads)),
            pl.BlockSpec(tab.shape, lambda h, i: (0, 0)),
        ],
        out_specs=pl.BlockSpec((step, dh), lambda h, i: (i, h)),
        out_shape=jax.ShapeDtypeStruct((rows, d), BF16),
        scratch_shapes=[pltpu.VMEM((step, dh), F32), pltpu.VMEM((step, LANES), F32)],
        compiler_params=_params("parallel", "arbitrary"),
        name="sb_attention",
    )(q, kv, kv, tab)


def _tiles(rows_padded):
    tm = ROW_TILE if rows_padded % ROW_TILE == 0 else SB_BLOCK
    return dict(tm=tm)


def kernel(x, meta_tokens, ffn_w_gu, ffn_w_down, ln_g, ln_b, gla_w_in, gla_w_g2, gla_b_g2, gla_norm_g,
           gla_w_out, sb_w_kv, sb_w_q, sb_w_out):
    batch, seq, d = x.shape
    assert batch == 1 and meta_tokens.shape == (N_META, d)
    depth = ffn_w_gu.shape[0]
    n_a = gla_w_in.shape[0]
    alpha = (2.0 * depth) ** 0.25
    rows = N_META + seq
    rows_p = -(-rows // ROW_ALIGN) * ROW_ALIGN
    tm = _tiles(rows_p)["tm"]

    qk = gla_w_g2.shape[2]
    vd = gla_w_out.shape[1]
    dk, dv = qk // GLA_HEADS, vd // GLA_HEADS
    dh = d // SB_HEADS

    h = jnp.concatenate([meta_tokens.astype(F32), x[0], jnp.zeros((rows_p - rows, d), F32)], axis=0)
    hb = h.astype(BF16)
    kv_shared = None

    w_gu_all = ffn_w_gu.astype(BF16)
    w_down_all = ffn_w_down.astype(BF16)

    def ffn(h, layer, half):
        return _ffn_ln(h, w_gu_all, w_down_all, (layer, half), ln_g[layer, 2 * half], ln_b[layer, 2 * half],
                       alpha=alpha, tm=tm, th=FFN_HIDDEN_TILE)

    for layer in range(depth):
        h, hb = ffn(h, layer, 0)
        if layer < n_a:
            w_in = gla_w_in[layer].astype(BF16)
            qkvr = _proj(hb, w_in, out_dtype=BF16, tm=tm, tn=PROJ_COL_TILE, n=2 * qk + 2 * vd)
            w_gate = jnp.pad(w_in[:, 2 * qk + 2 * vd:], ((0, 0), (0, LANES - GLA_GATE_RANK)))
            g_low = _proj(hb, w_gate, out_dtype=F32, tm=tm, tn=LANES)
            w_g2 = jnp.pad(gla_w_g2[layer].astype(BF16), ((0, LANES - GLA_GATE_RANK), (0, 0)))
            mixed = _gla(qkvr, g_low, w_g2, gla_b_g2[layer], gla_norm_g[layer],
                         heads=GLA_HEADS, dk=dk, dv=dv, chunk=GLA_CHUNK)
            w_out = gla_w_out[layer].astype(BF16)
        else:
            jb = layer - n_a
            q = _proj(hb, sb_w_q[jb].astype(BF16), out_dtype=BF16, tm=tm, tn=PROJ_COL_TILE, scale=dh ** -0.5)
            mixed = _sb_attention(q, kv_shared, heads=SB_HEADS, blk=SB_BLOCK, n_sub=SB_GROUP)
            w_out = sb_w_out[jb].astype(BF16)
        if layer == depth - 1 and layer != n_a - 1 and seq % FINAL_ROW_TILE == 0:
            tm = FINAL_ROW_TILE
            h, hb = _mix_ln(mixed, w_out, h, ln_g[layer, 1], ln_b[layer, 1], alpha=alpha, tm=tm,
                            first_row=N_META, rows=seq)
            h, hb = ffn(h, layer, 1)
            return h[None]
        h, hb = _mix_ln(mixed, w_out, h, ln_g[layer, 1], ln_b[layer, 1], alpha=alpha, tm=tm)
        h, hb = ffn(h, layer, 1)
        if layer == n_a - 1:
            kv_shared = _proj(hb, sb_w_kv.astype(BF16), out_dtype=BF16, tm=tm, tn=PROJ_COL_TILE)
    return h[N_META:rows][None]
```

```python
import functools
import math

import numpy as np
import jax
import jax.numpy as jnp
from jax import lax
from jax.experimental import pallas as pl
from jax.experimental.pallas import tpu as pltpu

F32 = jnp.float32
BF16 = jnp.bfloat16

N_META = 16
GLA_HEADS = 4
GLA_GATE_RANK = 16
GLA_GATE_TAU = 16.0
SB_HEADS = 16
LN_EPS = 1e-5
RMS_EPS = 1e-6

LANES = 128
VMEM_LIMIT_BYTES = 56 * 1024 * 1024

ROW_TILE = 640
FINAL_ROW_TILE = 512
ROW_ALIGN = 1280
FFN_HIDDEN_TILE = 512
PROJ_COL_TILE = 2048
EPILOGUE_SPLIT = 4
GLA_CHUNK = 128
SB_BLOCK = 256
SB_GROUP = ROW_ALIGN // SB_BLOCK
SB_HEADS_PER_STEP = 2
SB_LOG_WEIGHT_FLOOR = -105.0


def _dot(a, b):
    return jnp.dot(a, b, preferred_element_type=F32)


def _dot_nt(a, b):
    return lax.dot_general(a, b, (((1,), (1,)), ((), ())), preferred_element_type=F32)


def _dot_tn(a, b):
    return lax.dot_general(a, b, (((0,), (0,)), ((), ())), preferred_element_type=F32)


def _log_sigmoid(x):
    return -(jnp.maximum(-x, 0.0) + jnp.log1p(jnp.exp(-jnp.abs(x))))


def _silu(x):
    return x * jax.nn.sigmoid(x)


def _layer_norm(y, g, b):
    mu = jnp.mean(y, -1, keepdims=True)
    yc = y - mu
    var = jnp.mean(yc * yc, -1, keepdims=True)
    return yc * lax.rsqrt(var + LN_EPS) * g + b


def _params(*sem):
    return pltpu.CompilerParams(dimension_semantics=sem, vmem_limit_bytes=VMEM_LIMIT_BYTES)


def _ffn_ln_kernel(h_ref, wg_ref, wu_ref, wd_ref, g_ref, b_ref, o_ref, ob_ref, xb_ref, acc_ref, *, alpha):
    j = pl.program_id(1)

    @pl.when(j == 0)
    def _():
        xb_ref[...] = h_ref[...].astype(BF16)
        acc_ref[...] = jnp.zeros_like(acc_ref)

    x = xb_ref[...]
    gate = _dot(x, wg_ref[...])
    up = _dot(x, wu_ref[...])
    act = (_silu(gate) * up).astype(BF16)
    acc_ref[...] += _dot(act, wd_ref[...])

    @pl.when(j == pl.num_programs(1) - 1)
    def _():
        y = _layer_norm(alpha * h_ref[...] + 0.5 * acc_ref[...], g_ref[...], b_ref[...])
        o_ref[...] = y
        ob_ref[...] = y.astype(BF16)


def _ffn_ln(h, w_gu, w_down, which, g, b, *, alpha, tm, th):
    rows, d = h.shape
    hidden = w_down.shape[2]
    nh = hidden // th
    assert rows % tm == 0 and hidden % th == 0 and w_gu.shape[2:] == (d, 2 * hidden)
    return pl.pallas_call(
        functools.partial(_ffn_ln_kernel, alpha=alpha),
        grid=(rows // tm, nh),
        in_specs=[
            pl.BlockSpec((tm, d), lambda i, j: (i, 0)),
            pl.BlockSpec((None, None, d, th), lambda i, j: (*which, 0, j)),
            pl.BlockSpec((None, None, d, th), lambda i, j: (*which, 0, j + nh)),
            pl.BlockSpec((None, None, th, d), lambda i, j: (*which, j, 0)),
            pl.BlockSpec((1, d), lambda i, j: (0, 0)),
            pl.BlockSpec((1, d), lambda i, j: (0, 0)),
        ],
        out_specs=[
            pl.BlockSpec((tm, d), lambda i, j: (i, 0)),
            pl.BlockSpec((tm, d), lambda i, j: (i, 0)),
        ],
        out_shape=[jax.ShapeDtypeStruct((rows, d), F32), jax.ShapeDtypeStruct((rows, d), BF16)],
        scratch_shapes=[pltpu.VMEM((tm, d), BF16), pltpu.VMEM((tm, d), F32)],
        compiler_params=_params("parallel", "arbitrary"),
        name="ffn_ln",
    )(h, w_gu, w_gu, w_down, g.reshape(1, d), b.reshape(1, d))


def _proj_kernel(x_ref, w_ref, o_ref, *, scale):
    y = _dot(x_ref[...], w_ref[...])
    if scale != 1.0:
        y = y * scale
    o_ref[...] = y.astype(o_ref.dtype)


def _proj(x, w, *, out_dtype, tm, tn, scale=1.0, n=None):
    rows, k = x.shape
    n = w.shape[1] if n is None else n
    assert rows % tm == 0 and n % tn == 0 and w.shape[0] == k and n <= w.shape[1]
    tn = min(tn, n)
    return pl.pallas_call(
        functools.partial(_proj_kernel, scale=scale),
        grid=(n // tn, rows // tm),
        in_specs=[
            pl.BlockSpec((tm, k), lambda j, i: (i, 0)),
            pl.BlockSpec((k, tn), lambda j, i: (0, j)),
        ],
        out_specs=pl.BlockSpec((tm, tn), lambda j, i: (i, j)),
        out_shape=jax.ShapeDtypeStruct((rows, n), out_dtype),
        compiler_params=_params("parallel", "arbitrary"),
        name="proj",
    )(x, w)


def _mix_ln_kernel(a_ref, w_ref, h_ref, g_ref, b_ref, o_ref, ob_ref, *, alpha):
    part = a_ref.shape[0] // EPILOGUE_SPLIT
    for r in range(EPILOGUE_SPLIT):
        rows = slice(r * part, (r + 1) * part)
        mix = _dot(a_ref[rows, :], w_ref[...])
        y = _layer_norm(alpha * h_ref[rows, :] + mix, g_ref[...], b_ref[...])
        o_ref[rows, :] = y
        ob_ref[rows, :] = y.astype(BF16)


def _mix_ln(a, w, h, g, b, *, alpha, tm, first_row=0, rows=None):
    d = h.shape[1]
    rows = h.shape[0] if rows is None else rows
    k = a.shape[1]
    assert rows % tm == 0 and w.shape == (k, d) and first_row + rows <= h.shape[0] and a.shape[0] == h.shape[0]
    if first_row == 0:
        a_spec = pl.BlockSpec((tm, k), lambda i: (i, 0))
        h_spec = pl.BlockSpec((tm, d), lambda i: (i, 0))
    else:
        align = math.gcd(tm, first_row)

        def window(i):
            return pl.multiple_of(i * tm + first_row, align), 0

        a_spec = pl.BlockSpec((pl.Element(tm), pl.Element(k)), window)
        h_spec = pl.BlockSpec((pl.Element(tm), pl.Element(d)), window)
    return pl.pallas_call(
        functools.partial(_mix_ln_kernel, alpha=alpha),
        grid=(rows // tm,),
        in_specs=[
            a_spec,
            pl.BlockSpec((k, d), lambda i: (0, 0)),
            h_spec,
            pl.BlockSpec((1, d), lambda i: (0, 0)),
            pl.BlockSpec((1, d), lambda i: (0, 0)),
        ],
        out_specs=[
            pl.BlockSpec((tm, d), lambda i: (i, 0)),
            pl.BlockSpec((tm, d), lambda i: (i, 0)),
        ],
        out_shape=[jax.ShapeDtypeStruct((rows, d), F32), jax.ShapeDtypeStruct((rows, d), BF16)],
        compiler_params=_params("parallel"),
        name="mix_ln",
    )(a, w, h, g.reshape(1, d), b.reshape(1, d))


def _gla_tables(c):
    n_lev = int(math.log2(c))
    assert 1 << n_lev == c
    t = np.arange(c)[:, None]
    j = np.arange(c)[None, :]
    blocks, masks = [], [np.eye(c, dtype=np.float32)]
    for lev in range(n_lev):
        m = c >> (lev + 1)
        mid = (t // (2 * m)) * (2 * m) + m
        upper = t >= mid
        a = np.where(upper, (j >= mid) & (j <= t), (j > t) & (j <= mid - 1))
        blocks.append(a)
        s_mid = (j // (2 * m)) * (2 * m) + m
        masks.append(((t // (2 * m)) == (j // (2 * m))) & upper & (j < s_mid))
    blocks.append(j <= t)
    blocks.append(j > t)
    sums = np.concatenate(blocks, axis=0).astype(np.float32)
    sums = np.concatenate([sums, sums], axis=1)
    return n_lev, jnp.asarray(sums, BF16), jnp.asarray(np.stack(masks).astype(np.float32), F32)


def _gla_kernel(q_ref, k_ref, v_ref, r_ref, gl_ref, wg2_ref, bg2_ref, ng_ref, sums_ref, masks_ref,
                o_ref, st_ref, *, heads, dk, dv, n_lev, q_scale):
    c = q_ref.shape[0]

    @pl.when(pl.program_id(0) == 0)
    def _():
        st_ref[...] = jnp.zeros_like(st_ref)

    g_low = gl_ref[...].astype(BF16)
    for hd in range(heads):
        ks = slice(hd * dk, (hd + 1) * dk)
        vs = slice(hd * dv, (hd + 1) * dv)
        x = _dot(g_low, wg2_ref[:, ks]) + bg2_ref[:, ks]
        lg = _log_sigmoid(x) * (math.log2(math.e) / GLA_GATE_TAU)
        lg_hi = lg.astype(BF16)
        lg_lo = (lg - lg_hi.astype(F32)).astype(BF16)
        fac = jnp.exp2(_dot(sums_ref[...], jnp.concatenate([lg_hi, lg_lo], axis=0)))
        q = q_ref[:, ks].astype(F32) * q_scale
        k = k_ref[:, ks].astype(F32)
        v = v_ref[:, vs]
        scores = masks_ref[0] * _dot_nt(q.astype(BF16), k.astype(BF16))
        for lev in range(n_lev):
            f = fac[lev * c:(lev + 1) * c]
            scores = scores + masks_ref[lev + 1] * _dot_nt((q * f).astype(BF16), (k * f).astype(BF16))
        f_cum = fac[n_lev * c:(n_lev + 1) * c]
        f_dec = fac[(n_lev + 1) * c:(n_lev + 2) * c]
        state_t = st_ref[hd]
        o = _dot(scores.astype(BF16), v) + _dot_nt((q * f_cum).astype(BF16), state_t.astype(BF16))
        st_ref[hd] = state_t * f_cum[c - 1:c, :] + _dot_tn(v, (k * f_dec).astype(BF16))
        o = o * lax.rsqrt(jnp.mean(o * o, -1, keepdims=True) + RMS_EPS) * ng_ref[...]
        r = r_ref[:, vs].astype(F32)
        o_ref[:, vs] = (o * _silu(r)).astype(o_ref.dtype)


def _gla(qkvr, g_low, w_g2, b_g2, norm_g, *, heads, dk, dv, chunk):
    rows = qkvr.shape[0]
    qk, vd = heads * dk, heads * dv
    assert rows % chunk == 0 and qkvr.shape[1] == 2 * qk + 2 * vd and vd == 2 * qk
    n_lev, sums, masks = _gla_tables(chunk)
    rank_pad = g_low.shape[1]
    return pl.pallas_call(
        functools.partial(_gla_kernel, heads=heads, dk=dk, dv=dv, n_lev=n_lev, q_scale=dk ** -0.5),
        grid=(rows // chunk,),
        in_specs=[
            pl.BlockSpec((chunk, qk), lambda i: (i, 0)),
            pl.BlockSpec((chunk, qk), lambda i: (i, 1)),
            pl.BlockSpec((chunk, vd), lambda i: (i, 1)),
            pl.BlockSpec((chunk, vd), lambda i: (i, 2)),
            pl.BlockSpec((chunk, rank_pad), lambda i: (i, 0)),
            pl.BlockSpec((rank_pad, qk), lambda i: (0, 0)),
            pl.BlockSpec((1, qk), lambda i: (0, 0)),
            pl.BlockSpec((1, dv), lambda i: (0, 0)),
            pl.BlockSpec(sums.shape, lambda i: (0, 0)),
            pl.BlockSpec(masks.shape, lambda i: (0, 0, 0)),
        ],
        out_specs=pl.BlockSpec((chunk, vd), lambda i: (i, 0)),
        out_shape=jax.ShapeDtypeStruct((rows, vd), BF16),
        scratch_shapes=[pltpu.VMEM((heads, dv, dk), F32)],
        compiler_params=_params("arbitrary"),
        name="gla",
    )(qkvr, qkvr, qkvr, qkvr, g_low, w_g2, b_g2.reshape(1, qk), norm_g.reshape(1, dv), sums, masks)


def _sb_tables(tk):
    j = np.arange(tk)[:, None]
    s = np.arange(tk)[None, :]
    tab = np.concatenate([(j >= s), np.ones((tk, LANES), bool)], axis=1)
    return jnp.asarray(-tab.astype(np.float32), BF16)


def _sb_tile(q, k_tile, v_tile, tab, carry, acc, *, diagonal):
    blk = q.shape[0]
    z = _dot_nt(q, k_tile)
    softplus = jnp.maximum(z, 0.0) + jnp.log(1.0 + jnp.exp(-jnp.abs(z)))
    if diagonal:
        strict = (lax.broadcasted_iota(jnp.int32, (blk, blk), 1)
                  < lax.broadcasted_iota(jnp.int32, (blk, blk), 0))
        softplus = jnp.where(strict, softplus, 0.0)
    sums = _dot(softplus.astype(BF16), tab)
    logit = z + sums[:, :blk]
    if carry is not None:
        logit = logit + jnp.concatenate([carry] * (blk // LANES), axis=1)
    w = jnp.exp(logit)
    if diagonal:
        w = jnp.where(strict, w, 0.0)
    out = _dot(w.astype(BF16), v_tile)
    total = sums[:, blk:]
    return (total if carry is None else carry + total), (out if acc is None else acc + out)


def _sb_kernel(q_ref, k_ref, v_ref, tab_ref, o_ref, acc_ref, carry_ref, *, blk, n_sub, n_head, dh):
    base = pl.program_id(1) * n_sub
    tab = tab_ref[...]
    units = [(hd, s) for hd in range(n_head) for s in range(n_sub)]

    def rows_of(s):
        return pl.ds(s * blk, blk)

    def cols_of(hd):
        return slice(hd * dh, (hd + 1) * dh)

    def carry_rows(hd, s):
        return pl.ds((hd * n_sub + s) * blk, blk)

    def keys_of(j):
        return pl.ds(pl.multiple_of(j * blk, blk), blk)

    for hd, s in units:
        q = q_ref[rows_of(s), cols_of(hd)]
        carry, acc = _sb_tile(q, k_ref[keys_of(base + s), cols_of(hd)], v_ref[keys_of(base + s), cols_of(hd)],
                              tab, None, None, diagonal=True)
        below = jnp.maximum(base + s - 1, 0)
        carry2, acc2 = _sb_tile(q, k_ref[keys_of(below), cols_of(hd)], v_ref[keys_of(below), cols_of(hd)], tab,
                                carry, acc, diagonal=False)
        if s == 0:
            carry2 = jnp.where(base > 0, carry2, carry)
            acc2 = jnp.where(base > 0, acc2, acc)
        carry_ref[carry_rows(hd, s), :] = carry2
        acc_ref[rows_of(s), cols_of(hd)] = acc2

    def walk(hd, s, j):
        carry, acc = _sb_tile(q_ref[rows_of(s), cols_of(hd)], k_ref[keys_of(j), cols_of(hd)],
                              v_ref[keys_of(j), cols_of(hd)], tab,
                              carry_ref[carry_rows(hd, s), :], acc_ref[rows_of(s), cols_of(hd)], diagonal=False)
        carry_ref[carry_rows(hd, s), :] = carry
        acc_ref[rows_of(s), cols_of(hd)] = acc

    tops = [jnp.max(carry_ref[carry_rows(hd, s), :]) for hd, s in units]
    for (hd, s), top in zip(units, tops):
        def cond(state):
            j, top = state
            return jnp.logical_and(j >= 0, top > SB_LOG_WEIGHT_FLOOR)

        def body(state, hd=hd, s=s):
            j, _ = state
            walk(hd, s, j)
            return j - 1, jnp.max(carry_ref[carry_rows(hd, s), :])

        lax.while_loop(cond, body, (base + s - 2, top))
    o_ref[...] = acc_ref[...].astype(o_ref.dtype)


def _sb_attention(q, kv, *, heads, blk, n_sub):
    rows, d = q.shape
    dh = d // heads
    step = blk * n_sub
    n_head = SB_HEADS_PER_STEP
    width = n_head * dh
    groups = heads // n_head
    assert rows % step == 0 and dh % LANES == 0 and blk % LANES == 0 and kv.shape == (rows, 2 * d)
    assert heads % n_head == 0
    tab = _sb_tables(blk)
    return pl.pallas_call(
        functools.partial(_sb_kernel, blk=blk, n_sub=n_sub, n_head=n_head, dh=dh),
        grid=(groups, rows // step),
        in_specs=[
            pl.BlockSpec((step, width), lambda h, i: (i, h)),
            pl.BlockSpec((rows, width), lambda h, i: (0, h)),
            pl.BlockSpec((rows, width), lambda h, i: (0, h + groups)),
            pl.BlockSpec(tab.shape, lambda h, i: (0, 0)),
        ],
        out_specs=pl.BlockSpec((step, width), lambda h, i: (i, h)),
        out_shape=jax.ShapeDtypeStruct((rows, d), BF16),
        scratch_shapes=[pltpu.VMEM((step, width), F32), pltpu.VMEM((n_head * step, LANES), F32)],
        compiler_params=_params("parallel", "arbitrary"),
        name="sb_attention",
    )(q, kv, kv, tab)


def _tiles(rows_padded):
    tm = ROW_TILE if rows_padded % ROW_TILE == 0 else SB_BLOCK
    return dict(tm=tm)


def kernel(x, meta_tokens, ffn_w_gu, ffn_w_down, ln_g, ln_b, gla_w_in, gla_w_g2, gla_b_g2, gla_norm_g,
           gla_w_out, sb_w_kv, sb_w_q, sb_w_out):
    batch, seq, d = x.shape
    assert batch == 1 and meta_tokens.shape == (N_META, d)
    depth = ffn_w_gu.shape[0]
    n_a = gla_w_in.shape[0]
    alpha = (2.0 * depth) ** 0.25
    rows = N_META + seq
    rows_p = -(-rows // ROW_ALIGN) * ROW_ALIGN
    tm = _tiles(rows_p)["tm"]

    qk = gla_w_g2.shape[2]
    vd = gla_w_out.shape[1]
    dk, dv = qk // GLA_HEADS, vd // GLA_HEADS
    dh = d // SB_HEADS

    h = jnp.concatenate([meta_tokens.astype(F32), x[0], jnp.zeros((rows_p - rows, d), F32)], axis=0)
    hb = h.astype(BF16)
    kv_shared = None

    w_gu_all = ffn_w_gu.astype(BF16)
    w_down_all = ffn_w_down.astype(BF16)

    def ffn(h, layer, half):
        return _ffn_ln(h, w_gu_all, w_down_all, (layer, half), ln_g[layer, 2 * half], ln_b[layer, 2 * half],
                       alpha=alpha, tm=tm, th=FFN_HIDDEN_TILE)

    for layer in range(depth):
        h, hb = ffn(h, layer, 0)
        if layer < n_a:
            w_in = gla_w_in[layer].astype(BF16)
            qkvr = _proj(hb, w_in, out_dtype=BF16, tm=tm, tn=PROJ_COL_TILE, n=2 * qk + 2 * vd)
            w_gate = jnp.pad(w_in[:, 2 * qk + 2 * vd:], ((0, 0), (0, LANES - GLA_GATE_RANK)))
            g_low = _proj(hb, w_gate, out_dtype=F32, tm=tm, tn=LANES)
            w_g2 = jnp.pad(gla_w_g2[layer].astype(BF16), ((0, LANES - GLA_GATE_RANK), (0, 0)))
            mixed = _gla(qkvr, g_low, w_g2, gla_b_g2[layer], gla_norm_g[layer],
                         heads=GLA_HEADS, dk=dk, dv=dv, chunk=GLA_CHUNK)
            w_out = gla_w_out[layer].astype(BF16)
        else:
            jb = layer - n_a
            q = _proj(hb, sb_w_q[jb].astype(BF16), out_dtype=BF16, tm=tm, tn=PROJ_COL_TILE, scale=dh ** -0.5)
            mixed = _sb_attention(q, kv_shared, heads=SB_HEADS, blk=SB_BLOCK, n_sub=SB_GROUP)
            w_out = sb_w_out[jb].astype(BF16)
        if layer == depth - 1 and layer != n_a - 1 and seq % FINAL_ROW_TILE == 0:
            tm = FINAL_ROW_TILE
            h, hb = _mix_ln(mixed, w_out, h, ln_g[layer, 1], ln_b[layer, 1], alpha=alpha, tm=tm,
                            first_row=N_META, rows=seq)
            h, hb = ffn(h, layer, 1)
            return h[None]
        h, hb = _mix_ln(mixed, w_out, h, ln_g[layer, 1], ln_b[layer, 1], alpha=alpha, tm=tm)
        h, hb = ffn(h, layer, 1)
        if layer == n_a - 1:
            kv_shared = _proj(hb, sb_w_kv.astype(BF16), out_dtype=BF16, tm=tm, tn=PROJ_COL_TILE)
    return h[N_META:rows][None]
```

```python
import functools
import math

import numpy as np
import jax
import jax.numpy as jnp
from jax import lax
from jax.experimental import pallas as pl
from jax.experimental.pallas import tpu as pltpu

F32 = jnp.float32
BF16 = jnp.bfloat16

N_META = 16
GLA_HEADS = 4
GLA_GATE_RANK = 16
GLA_GATE_TAU = 16.0
SB_HEADS = 16
LN_EPS = 1e-5
RMS_EPS = 1e-6

LANES = 128
VMEM_LIMIT_BYTES = 56 * 1024 * 1024

ROW_TILE = 640
FINAL_ROW_TILE = 512
ROW_ALIGN = 1280
FFN_HIDDEN_TILE = 512
PROJ_COL_TILE = 2048
EPILOGUE_SPLIT = 4
GLA_CHUNK = 128
SB_BLOCK = 256
SB_GROUP = ROW_ALIGN // SB_BLOCK
SB_HEADS_PER_STEP = 2
SB_LOG_WEIGHT_FLOOR = -105.0
SB_NO_KEYS_LEFT = -1e30


def _dot(a, b):
    return jnp.dot(a, b, preferred_element_type=F32)


def _dot_nt(a, b):
    return lax.dot_general(a, b, (((1,), (1,)), ((), ())), preferred_element_type=F32)


def _dot_tn(a, b):
    return lax.dot_general(a, b, (((0,), (0,)), ((), ())), preferred_element_type=F32)


def _log_sigmoid(x):
    return -(jnp.maximum(-x, 0.0) + jnp.log1p(jnp.exp(-jnp.abs(x))))


def _silu(x):
    return x * jax.nn.sigmoid(x)


def _layer_norm(y, g, b):
    mu = jnp.mean(y, -1, keepdims=True)
    yc = y - mu
    var = jnp.mean(yc * yc, -1, keepdims=True)
    return yc * lax.rsqrt(var + LN_EPS) * g + b


def _params(*sem):
    return pltpu.CompilerParams(dimension_semantics=sem, vmem_limit_bytes=VMEM_LIMIT_BYTES)


def _ffn_ln_kernel(h_ref, wg_ref, wu_ref, wd_ref, g_ref, b_ref, o_ref, ob_ref, xb_ref, acc_ref, *, alpha):
    j = pl.program_id(1)

    @pl.when(j == 0)
    def _():
        xb_ref[...] = h_ref[...].astype(BF16)
        acc_ref[...] = jnp.zeros_like(acc_ref)

    x = xb_ref[...]
    gate = _dot(x, wg_ref[...])
    up = _dot(x, wu_ref[...])
    act = (_silu(gate) * up).astype(BF16)
    acc_ref[...] += _dot(act, wd_ref[...])

    @pl.when(j == pl.num_programs(1) - 1)
    def _():
        y = _layer_norm(alpha * h_ref[...] + 0.5 * acc_ref[...], g_ref[...], b_ref[...])
        o_ref[...] = y
        ob_ref[...] = y.astype(BF16)


def _ffn_ln(h, w_gu, w_down, which, g, b, *, alpha, tm, th):
    rows, d = h.shape
    hidden = w_down.shape[2]
    nh = hidden // th
    assert rows % tm == 0 and hidden % th == 0 and w_gu.shape[2:] == (d, 2 * hidden)
    return pl.pallas_call(
        functools.partial(_ffn_ln_kernel, alpha=alpha),
        grid=(rows // tm, nh),
        in_specs=[
            pl.BlockSpec((tm, d), lambda i, j: (i, 0)),
            pl.BlockSpec((None, None, d, th), lambda i, j: (*which, 0, j)),
            pl.BlockSpec((None, None, d, th), lambda i, j: (*which, 0, j + nh)),
            pl.BlockSpec((None, None, th, d), lambda i, j: (*which, j, 0)),
            pl.BlockSpec((1, d), lambda i, j: (0, 0)),
            pl.BlockSpec((1, d), lambda i, j: (0, 0)),
        ],
        out_specs=[
            pl.BlockSpec((tm, d), lambda i, j: (i, 0)),
            pl.BlockSpec((tm, d), lambda i, j: (i, 0)),
        ],
        out_shape=[jax.ShapeDtypeStruct((rows, d), F32), jax.ShapeDtypeStruct((rows, d), BF16)],
        scratch_shapes=[pltpu.VMEM((tm, d), BF16), pltpu.VMEM((tm, d), F32)],
        compiler_params=_params("parallel", "arbitrary"),
        name="ffn_ln",
    )(h, w_gu, w_gu, w_down, g.reshape(1, d), b.reshape(1, d))


def _proj_kernel(x_ref, w_ref, o_ref, *, scale):
    y = _dot(x_ref[...], w_ref[...])
    if scale != 1.0:
        y = y * scale
    o_ref[...] = y.astype(o_ref.dtype)


def _proj(x, w, *, out_dtype, tm, tn, scale=1.0, n=None):
    rows, k = x.shape
    n = w.shape[1] if n is None else n
    assert rows % tm == 0 and n % tn == 0 and w.shape[0] == k and n <= w.shape[1]
    tn = min(tn, n)
    return pl.pallas_call(
        functools.partial(_proj_kernel, scale=scale),
        grid=(n // tn, rows // tm),
        in_specs=[
            pl.BlockSpec((tm, k), lambda j, i: (i, 0)),
            pl.BlockSpec((k, tn), lambda j, i: (0, j)),
        ],
        out_specs=pl.BlockSpec((tm, tn), lambda j, i: (i, j)),
        out_shape=jax.ShapeDtypeStruct((rows, n), out_dtype),
        compiler_params=_params("parallel", "arbitrary"),
        name="proj",
    )(x, w)


def _mix_ln_kernel(a_ref, w_ref, h_ref, g_ref, b_ref, o_ref, ob_ref, *, alpha):
    part = a_ref.shape[0] // EPILOGUE_SPLIT
    for r in range(EPILOGUE_SPLIT):
        rows = slice(r * part, (r + 1) * part)
        mix = _dot(a_ref[rows, :], w_ref[...])
        y = _layer_norm(alpha * h_ref[rows, :] + mix, g_ref[...], b_ref[...])
        o_ref[rows, :] = y
        ob_ref[rows, :] = y.astype(BF16)


def _mix_ln(a, w, h, g, b, *, alpha, tm, first_row=0, rows=None):
    d = h.shape[1]
    rows = h.shape[0] if rows is None else rows
    k = a.shape[1]
    assert rows % tm == 0 and w.shape == (k, d) and first_row + rows <= h.shape[0] and a.shape[0] == h.shape[0]
    if first_row == 0:
        a_spec = pl.BlockSpec((tm, k), lambda i: (i, 0))
        h_spec = pl.BlockSpec((tm, d), lambda i: (i, 0))
    else:
        align = math.gcd(tm, first_row)

        def window(i):
            return pl.multiple_of(i * tm + first_row, align), 0

        a_spec = pl.BlockSpec((pl.Element(tm), pl.Element(k)), window)
        h_spec = pl.BlockSpec((pl.Element(tm), pl.Element(d)), window)
    return pl.pallas_call(
        functools.partial(_mix_ln_kernel, alpha=alpha),
        grid=(rows // tm,),
        in_specs=[
            a_spec,
            pl.BlockSpec((k, d), lambda i: (0, 0)),
            h_spec,
            pl.BlockSpec((1, d), lambda i: (0, 0)),
            pl.BlockSpec((1, d), lambda i: (0, 0)),
        ],
        out_specs=[
            pl.BlockSpec((tm, d), lambda i: (i, 0)),
            pl.BlockSpec((tm, d), lambda i: (i, 0)),
        ],
        out_shape=[jax.ShapeDtypeStruct((rows, d), F32), jax.ShapeDtypeStruct((rows, d), BF16)],
        compiler_params=_params("parallel"),
        name="mix_ln",
    )(a, w, h, g.reshape(1, d), b.reshape(1, d))


def _gla_tables(c):
    n_lev = int(math.log2(c))
    assert 1 << n_lev == c
    t = np.arange(c)[:, None]
    j = np.arange(c)[None, :]
    blocks, masks = [], [np.eye(c, dtype=np.float32)]
    for lev in range(n_lev):
        m = c >> (lev + 1)
        mid = (t // (2 * m)) * (2 * m) + m
        upper = t >= mid
        a = np.where(upper, (j >= mid) & (j <= t), (j > t) & (j <= mid - 1))
        blocks.append(a)
        s_mid = (j // (2 * m)) * (2 * m) + m
        masks.append(((t // (2 * m)) == (j // (2 * m))) & upper & (j < s_mid))
    blocks.append(j <= t)
    blocks.append(j > t)
    sums = np.concatenate(blocks, axis=0).astype(np.float32)
    sums = np.concatenate([sums, sums], axis=1)
    return n_lev, jnp.asarray(sums, BF16), jnp.asarray(np.stack(masks).astype(np.float32), F32)


def _gla_kernel(q_ref, k_ref, v_ref, r_ref, gl_ref, wg2_ref, bg2_ref, ng_ref, sums_ref, masks_ref,
                o_ref, st_ref, *, heads, dk, dv, n_lev, q_scale):
    c = q_ref.shape[0]

    @pl.when(pl.program_id(0) == 0)
    def _():
        st_ref[...] = jnp.zeros_like(st_ref)

    g_low = gl_ref[...].astype(BF16)
    for hd in range(heads):
        ks = slice(hd * dk, (hd + 1) * dk)
        vs = slice(hd * dv, (hd + 1) * dv)
        x = _dot(g_low, wg2_ref[:, ks]) + bg2_ref[:, ks]
        lg = _log_sigmoid(x) * (math.log2(math.e) / GLA_GATE_TAU)
        lg_hi = lg.astype(BF16)
        lg_lo = (lg - lg_hi.astype(F32)).astype(BF16)
        fac = jnp.exp2(_dot(sums_ref[...], jnp.concatenate([lg_hi, lg_lo], axis=0)))
        q = q_ref[:, ks].astype(F32) * q_scale
        k = k_ref[:, ks].astype(F32)
        v = v_ref[:, vs]
        scores = masks_ref[0] * _dot_nt(q.astype(BF16), k.astype(BF16))
        for lev in range(n_lev):
            f = fac[lev * c:(lev + 1) * c]
            scores = scores + masks_ref[lev + 1] * _dot_nt((q * f).astype(BF16), (k * f).astype(BF16))
        f_cum = fac[n_lev * c:(n_lev + 1) * c]
        f_dec = fac[(n_lev + 1) * c:(n_lev + 2) * c]
        state_t = st_ref[hd]
        o = _dot(scores.astype(BF16), v) + _dot_nt((q * f_cum).astype(BF16), state_t.astype(BF16))
        st_ref[hd] = state_t * f_cum[c - 1:c, :] + _dot_tn(v, (k * f_dec).astype(BF16))
        o = o * lax.rsqrt(jnp.mean(o * o, -1, keepdims=True) + RMS_EPS) * ng_ref[...]
        r = r_ref[:, vs].astype(F32)
        o_ref[:, vs] = (o * _silu(r)).astype(o_ref.dtype)


def _gla(qkvr, g_low, w_g2, b_g2, norm_g, *, heads, dk, dv, chunk):
    rows = qkvr.shape[0]
    qk, vd = heads * dk, heads * dv
    assert rows % chunk == 0 and qkvr.shape[1] == 2 * qk + 2 * vd and vd == 2 * qk
    n_lev, sums, masks = _gla_tables(chunk)
    rank_pad = g_low.shape[1]
    return pl.pallas_call(
        functools.partial(_gla_kernel, heads=heads, dk=dk, dv=dv, n_lev=n_lev, q_scale=dk ** -0.5),
        grid=(rows // chunk,),
        in_specs=[
            pl.BlockSpec((chunk, qk), lambda i: (i, 0)),
            pl.BlockSpec((chunk, qk), lambda i: (i, 1)),
            pl.BlockSpec((chunk, vd), lambda i: (i, 1)),
            pl.BlockSpec((chunk, vd), lambda i: (i, 2)),
            pl.BlockSpec((chunk, rank_pad), lambda i: (i, 0)),
            pl.BlockSpec((rank_pad, qk), lambda i: (0, 0)),
            pl.BlockSpec((1, qk), lambda i: (0, 0)),
            pl.BlockSpec((1, dv), lambda i: (0, 0)),
            pl.BlockSpec(sums.shape, lambda i: (0, 0)),
            pl.BlockSpec(masks.shape, lambda i: (0, 0, 0)),
        ],
        out_specs=pl.BlockSpec((chunk, vd), lambda i: (i, 0)),
        out_shape=jax.ShapeDtypeStruct((rows, vd), BF16),
        scratch_shapes=[pltpu.VMEM((heads, dv, dk), F32)],
        compiler_params=_params("arbitrary"),
        name="gla",
    )(qkvr, qkvr, qkvr, qkvr, g_low, w_g2, b_g2.reshape(1, qk), norm_g.reshape(1, dv), sums, masks)


def _sb_tables(tk):
    j = np.arange(tk)[:, None]
    s = np.arange(tk)[None, :]
    tab = np.concatenate([(j >= s), np.ones((tk, LANES), bool)], axis=1)
    return jnp.asarray(-tab.astype(np.float32), BF16)


def _sb_tile(q, k_tile, v_tile, tab, carry, acc, *, diagonal):
    blk = q.shape[0]
    z = _dot_nt(q, k_tile)
    softplus = jnp.maximum(z, 0.0) + jnp.log(1.0 + jnp.exp(-jnp.abs(z)))
    if diagonal:
        strict = (lax.broadcasted_iota(jnp.int32, (blk, blk), 1)
                  < lax.broadcasted_iota(jnp.int32, (blk, blk), 0))
        softplus = jnp.where(strict, softplus, 0.0)
    sums = _dot(softplus.astype(BF16), tab)
    logit = z + sums[:, :blk]
    if carry is not None:
        logit = logit + jnp.concatenate([carry] * (blk // LANES), axis=1)
    w = jnp.exp(logit)
    if diagonal:
        w = jnp.where(strict, w, 0.0)
    out = _dot(w.astype(BF16), v_tile)
    total = sums[:, blk:]
    return (total if carry is None else carry + total), (out if acc is None else acc + out)


def _sb_kernel(q_ref, k_ref, v_ref, tab_ref, o_ref, acc_ref, carry_ref, *, blk, n_sub, n_head, dh):
    base = pl.program_id(1) * n_sub
    tab = tab_ref[...]
    units = [(hd, s) for hd in range(n_head) for s in range(n_sub)]

    def rows_of(s):
        return pl.ds(s * blk, blk)

    def cols_of(hd):
        return slice(hd * dh, (hd + 1) * dh)

    def carry_rows(hd, s):
        return pl.ds((hd * n_sub + s) * blk, blk)

    def keys_of(j):
        return pl.ds(pl.multiple_of(j * blk, blk), blk)

    for hd, s in units:
        q = q_ref[rows_of(s), cols_of(hd)]
        carry, acc = _sb_tile(q, k_ref[keys_of(base + s), cols_of(hd)], v_ref[keys_of(base + s), cols_of(hd)],
                              tab, None, None, diagonal=True)
        below = jnp.maximum(base + s - 1, 0)
        carry2, acc2 = _sb_tile(q, k_ref[keys_of(below), cols_of(hd)], v_ref[keys_of(below), cols_of(hd)], tab,
                                carry, acc, diagonal=False)
        if s == 0:
            carry2 = jnp.where(base > 0, carry2, carry)
            acc2 = jnp.where(base > 0, acc2, acc)
        carry_ref[carry_rows(hd, s), :] = carry2
        acc_ref[rows_of(s), cols_of(hd)] = acc2

    def walk(hd, s, j):
        carry, acc = _sb_tile(q_ref[rows_of(s), cols_of(hd)], k_ref[keys_of(j), cols_of(hd)],
                              v_ref[keys_of(j), cols_of(hd)], tab,
                              carry_ref[carry_rows(hd, s), :], acc_ref[rows_of(s), cols_of(hd)], diagonal=False)
        carry_ref[carry_rows(hd, s), :] = carry
        acc_ref[rows_of(s), cols_of(hd)] = acc

    def top_carry(depth):
        top = None
        for hd, s in units:
            c = jnp.where(base + s - depth >= 0, carry_ref[carry_rows(hd, s), :], SB_NO_KEYS_LEFT)
            top = c if top is None else jnp.maximum(top, c)
        return jnp.max(top)

    def cond(state):
        _, top = state
        return top > SB_LOG_WEIGHT_FLOOR

    def body(state):
        depth, _ = state
        for hd, s in units:
            j = base + s - depth
            need = jnp.logical_and(j >= 0, jnp.max(carry_ref[carry_rows(hd, s), :]) > SB_LOG_WEIGHT_FLOOR)

            @pl.when(need)
            def _(hd=hd, s=s, j=j):
                walk(hd, s, j)
        return depth + 1, top_carry(depth + 1)

    lax.while_loop(cond, body, (2, top_carry(2)))
    o_ref[...] = acc_ref[...].astype(o_ref.dtype)


def _sb_attention(q, kv, *, heads, blk, n_sub):
    rows, d = q.shape
    dh = d // heads
    step = blk * n_sub
    n_head = SB_HEADS_PER_STEP
    width = n_head * dh
    groups = heads // n_head
    assert rows % step == 0 and dh % LANES == 0 and blk % LANES == 0 and kv.shape == (rows, 2 * d)
    assert heads % n_head == 0
    tab = _sb_tables(blk)
    return pl.pallas_call(
        functools.partial(_sb_kernel, blk=blk, n_sub=n_sub, n_head=n_head, dh=dh),
        grid=(groups, rows // step),
        in_specs=[
            pl.BlockSpec((step, width), lambda h, i: (i, h)),
            pl.BlockSpec((rows, width), lambda h, i: (0, h)),
            pl.BlockSpec((rows, width), lambda h, i: (0, h + groups)),
            pl.BlockSpec(tab.shape, lambda h, i: (0, 0)),
        ],
        out_specs=pl.BlockSpec((step, width), lambda h, i: (i, h)),
        out_shape=jax.ShapeDtypeStruct((rows, d), BF16),
        scratch_shapes=[pltpu.VMEM((step, width), F32), pltpu.VMEM((n_head * step, LANES), F32)],
        compiler_params=_params("parallel", "arbitrary"),
        name="sb_attention",
    )(q, kv, kv, tab)


def _tiles(rows_padded):
    tm = ROW_TILE if rows_padded % ROW_TILE == 0 else SB_BLOCK
    return dict(tm=tm)


def kernel(x, meta_tokens, ffn_w_gu, ffn_w_down, ln_g, ln_b, gla_w_in, gla_w_g2, gla_b_g2, gla_norm_g,
           gla_w_out, sb_w_kv, sb_w_q, sb_w_out):
    batch, seq, d = x.shape
    assert batch == 1 and meta_tokens.shape == (N_META, d)
    depth = ffn_w_gu.shape[0]
    n_a = gla_w_in.shape[0]
    alpha = (2.0 * depth) ** 0.25
    rows = N_META + seq
    rows_p = -(-rows // ROW_ALIGN) * ROW_ALIGN
    tm = _tiles(rows_p)["tm"]

    qk = gla_w_g2.shape[2]
    vd = gla_w_out.shape[1]
    dk, dv = qk // GLA_HEADS, vd // GLA_HEADS
    dh = d // SB_HEADS

    h = jnp.concatenate([meta_tokens.astype(F32), x[0], jnp.zeros((rows_p - rows, d), F32)], axis=0)
    hb = h.astype(BF16)
    kv_shared = None

    w_gu_all = ffn_w_gu.astype(BF16)
    w_down_all = ffn_w_down.astype(BF16)

    def ffn(h, layer, half):
        return _ffn_ln(h, w_gu_all, w_down_all, (layer, half), ln_g[layer, 2 * half], ln_b[layer, 2 * half],
                       alpha=alpha, tm=tm, th=FFN_HIDDEN_TILE)

    for layer in range(depth):
        h, hb = ffn(h, layer, 0)
        if layer < n_a:
            w_in = gla_w_in[layer].astype(BF16)
            qkvr = _proj(hb, w_in, out_dtype=BF16, tm=tm, tn=PROJ_COL_TILE, n=2 * qk + 2 * vd)
            w_gate = jnp.pad(w_in[:, 2 * qk + 2 * vd:], ((0, 0), (0, LANES - GLA_GATE_RANK)))
            g_low = _proj(hb, w_gate, out_dtype=F32, tm=tm, tn=LANES)
            w_g2 = jnp.pad(gla_w_g2[layer].astype(BF16), ((0, LANES - GLA_GATE_RANK), (0, 0)))
            mixed = _gla(qkvr, g_low, w_g2, gla_b_g2[layer], gla_norm_g[layer],
                         heads=GLA_HEADS, dk=dk, dv=dv, chunk=GLA_CHUNK)
            w_out = gla_w_out[layer].astype(BF16)
        else:
            jb = layer - n_a
            q = _proj(hb, sb_w_q[jb].astype(BF16), out_dtype=BF16, tm=tm, tn=PROJ_COL_TILE, scale=dh ** -0.5)
            mixed = _sb_attention(q, kv_shared, heads=SB_HEADS, blk=SB_BLOCK, n_sub=SB_GROUP)
            w_out = sb_w_out[jb].astype(BF16)
        if layer == depth - 1 and layer != n_a - 1 and seq % FINAL_ROW_TILE == 0:
            tm = FINAL_ROW_TILE
            h, hb = _mix_ln(mixed, w_out, h, ln_g[layer, 1], ln_b[layer, 1], alpha=alpha, tm=tm,
                            first_row=N_META, rows=seq)
            h, hb = ffn(h, layer, 1)
            return h[None]
        h, hb = _mix_ln(mixed, w_out, h, ln_g[layer, 1], ln_b[layer, 1], alpha=alpha, tm=tm)
        h, hb = ffn(h, layer, 1)
        if layer == n_a - 1:
            kv_shared = _proj(hb, sb_w_kv.astype(BF16), out_dtype=BF16, tm=tm, tn=PROJ_COL_TILE)
    return h[N_META:rows][None]
```

```python
import functools
import math

import numpy as np
import jax
import jax.numpy as jnp
from jax import lax
from jax.experimental import pallas as pl
from jax.experimental.pallas import tpu as pltpu

F32 = jnp.float32
BF16 = jnp.bfloat16

N_META = 16
GLA_HEADS = 4
GLA_GATE_RANK = 16
GLA_GATE_TAU = 16.0
SB_HEADS = 16
LN_EPS = 1e-5
RMS_EPS = 1e-6

LANES = 128
VMEM_LIMIT_BYTES = 56 * 1024 * 1024

ROW_TILE = 640
FINAL_ROW_TILE = 512
ROW_ALIGN = 1280
FFN_HIDDEN_TILE = 512
PROJ_COL_TILE = 2048
EPILOGUE_SPLIT = 4
GLA_CHUNK = 128
SB_BLOCK = 256
SB_GROUP = ROW_ALIGN // SB_BLOCK
SB_HEADS_PER_STEP = 2
SB_LOG_WEIGHT_FLOOR = -105.0


def _dot(a, b):
    return jnp.dot(a, b, preferred_element_type=F32)


def _dot_nt(a, b):
    return lax.dot_general(a, b, (((1,), (1,)), ((), ())), preferred_element_type=F32)


def _dot_tn(a, b):
    return lax.dot_general(a, b, (((0,), (0,)), ((), ())), preferred_element_type=F32)


def _log_sigmoid(x):
    return -(jnp.maximum(-x, 0.0) + jnp.log1p(jnp.exp(-jnp.abs(x))))


def _silu(x):
    return x * jax.nn.sigmoid(x)


def _layer_norm(y, g, b):
    mu = jnp.mean(y, -1, keepdims=True)
    yc = y - mu
    var = jnp.mean(yc * yc, -1, keepdims=True)
    return yc * lax.rsqrt(var + LN_EPS) * g + b


def _params(*sem):
    return pltpu.CompilerParams(dimension_semantics=sem, vmem_limit_bytes=VMEM_LIMIT_BYTES)


def _ffn_ln_kernel(h_ref, wg_ref, wu_ref, wd_ref, g_ref, b_ref, o_ref, ob_ref, xb_ref, acc_ref, *, alpha):
    j = pl.program_id(1)

    @pl.when(j == 0)
    def _():
        xb_ref[...] = h_ref[...].astype(BF16)
        acc_ref[...] = jnp.zeros_like(acc_ref)

    x = xb_ref[...]
    gate = _dot(x, wg_ref[...])
    up = _dot(x, wu_ref[...])
    act = (_silu(gate) * up).astype(BF16)
    acc_ref[...] += _dot(act, wd_ref[...])

    @pl.when(j == pl.num_programs(1) - 1)
    def _():
        y = _layer_norm(alpha * h_ref[...] + 0.5 * acc_ref[...], g_ref[...], b_ref[...])
        o_ref[...] = y
        ob_ref[...] = y.astype(BF16)


def _ffn_ln(h, w_gu, w_down, which, g, b, *, alpha, tm, th):
    rows, d = h.shape
    hidden = w_down.shape[2]
    nh = hidden // th
    assert rows % tm == 0 and hidden % th == 0 and w_gu.shape[2:] == (d, 2 * hidden)
    return pl.pallas_call(
        functools.partial(_ffn_ln_kernel, alpha=alpha),
        grid=(rows // tm, nh),
        in_specs=[
            pl.BlockSpec((tm, d), lambda i, j: (i, 0)),
            pl.BlockSpec((None, None, d, th), lambda i, j: (*which, 0, j)),
            pl.BlockSpec((None, None, d, th), lambda i, j: (*which, 0, j + nh)),
            pl.BlockSpec((None, None, th, d), lambda i, j: (*which, j, 0)),
            pl.BlockSpec((1, d), lambda i, j: (0, 0)),
            pl.BlockSpec((1, d), lambda i, j: (0, 0)),
        ],
        out_specs=[
            pl.BlockSpec((tm, d), lambda i, j: (i, 0)),
            pl.BlockSpec((tm, d), lambda i, j: (i, 0)),
        ],
        out_shape=[jax.ShapeDtypeStruct((rows, d), F32), jax.ShapeDtypeStruct((rows, d), BF16)],
        scratch_shapes=[pltpu.VMEM((tm, d), BF16), pltpu.VMEM((tm, d), F32)],
        compiler_params=_params("parallel", "arbitrary"),
        name="ffn_ln",
    )(h, w_gu, w_gu, w_down, g.reshape(1, d), b.reshape(1, d))


def _proj_kernel(x_ref, w_ref, o_ref, *, scale):
    y = _dot(x_ref[...], w_ref[...])
    if scale != 1.0:
        y = y * scale
    o_ref[...] = y.astype(o_ref.dtype)


def _proj(x, w, *, out_dtype, tm, tn, scale=1.0, n=None):
    rows, k = x.shape
    n = w.shape[1] if n is None else n
    assert rows % tm == 0 and n % tn == 0 and w.shape[0] == k and n <= w.shape[1]
    tn = min(tn, n)
    return pl.pallas_call(
        functools.partial(_proj_kernel, scale=scale),
        grid=(n // tn, rows // tm),
        in_specs=[
            pl.BlockSpec((tm, k), lambda j, i: (i, 0)),
            pl.BlockSpec((k, tn), lambda j, i: (0, j)),
        ],
        out_specs=pl.BlockSpec((tm, tn), lambda j, i: (i, j)),
        out_shape=jax.ShapeDtypeStruct((rows, n), out_dtype),
        compiler_params=_params("parallel", "arbitrary"),
        name="proj",
    )(x, w)


def _mix_ln_kernel(a_ref, w_ref, h_ref, g_ref, b_ref, o_ref, ob_ref, *, alpha):
    part = a_ref.shape[0] // EPILOGUE_SPLIT
    for r in range(EPILOGUE_SPLIT):
        rows = slice(r * part, (r + 1) * part)
        mix = _dot(a_ref[rows, :], w_ref[...])
        y = _layer_norm(alpha * h_ref[rows, :] + mix, g_ref[...], b_ref[...])
        o_ref[rows, :] = y
        ob_ref[rows, :] = y.astype(BF16)


def _mix_ln(a, w, h, g, b, *, alpha, tm, first_row=0, rows=None):
    d = h.shape[1]
    rows = h.shape[0] if rows is None else rows
    k = a.shape[1]
    assert rows % tm == 0 and w.shape == (k, d) and first_row + rows <= h.shape[0] and a.shape[0] == h.shape[0]
    if first_row == 0:
        a_spec = pl.BlockSpec((tm, k), lambda i: (i, 0))
        h_spec = pl.BlockSpec((tm, d), lambda i: (i, 0))
    else:
        align = math.gcd(tm, first_row)

        def window(i):
            return pl.multiple_of(i * tm + first_row, align), 0

        a_spec = pl.BlockSpec((pl.Element(tm), pl.Element(k)), window)
        h_spec = pl.BlockSpec((pl.Element(tm), pl.Element(d)), window)
    return pl.pallas_call(
        functools.partial(_mix_ln_kernel, alpha=alpha),
        grid=(rows // tm,),
        in_specs=[
            a_spec,
            pl.BlockSpec((k, d), lambda i: (0, 0)),
            h_spec,
            pl.BlockSpec((1, d), lambda i: (0, 0)),
            pl.BlockSpec((1, d), lambda i: (0, 0)),
        ],
        out_specs=[
            pl.BlockSpec((tm, d), lambda i: (i, 0)),
            pl.BlockSpec((tm, d), lambda i: (i, 0)),
        ],
        out_shape=[jax.ShapeDtypeStruct((rows, d), F32), jax.ShapeDtypeStruct((rows, d), BF16)],
        compiler_params=_params("parallel"),
        name="mix_ln",
    )(a, w, h, g.reshape(1, d), b.reshape(1, d))


def _gla_tables(c):
    n_lev = int(math.log2(c))
    assert 1 << n_lev == c
    t = np.arange(c)[:, None]
    j = np.arange(c)[None, :]
    blocks, masks = [], [np.eye(c, dtype=np.float32)]
    for lev in range(n_lev):
        m = c >> (lev + 1)
        mid = (t // (2 * m)) * (2 * m) + m
        upper = t >= mid
        a = np.where(upper, (j >= mid) & (j <= t), (j > t) & (j <= mid - 1))
        blocks.append(a)
        s_mid = (j // (2 * m)) * (2 * m) + m
        masks.append(((t // (2 * m)) == (j // (2 * m))) & upper & (j < s_mid))
    blocks.append(j <= t)
    blocks.append(j > t)
    sums = np.concatenate(blocks, axis=0).astype(np.float32)
    sums = np.concatenate([sums, sums], axis=1)
    return n_lev, jnp.asarray(sums, BF16), jnp.asarray(np.stack(masks).astype(np.float32), F32)


def _gla_kernel(q_ref, k_ref, v_ref, r_ref, gl_ref, wg2_ref, bg2_ref, ng_ref, sums_ref, masks_ref,
                o_ref, st_ref, *, heads, dk, dv, n_lev, q_scale):
    c = q_ref.shape[0]

    @pl.when(pl.program_id(0) == 0)
    def _():
        st_ref[...] = jnp.zeros_like(st_ref)

    g_low = gl_ref[...].astype(BF16)
    for hd in range(heads):
        ks = slice(hd * dk, (hd + 1) * dk)
        vs = slice(hd * dv, (hd + 1) * dv)
        x = _dot(g_low, wg2_ref[:, ks]) + bg2_ref[:, ks]
        lg = _log_sigmoid(x) * (math.log2(math.e) / GLA_GATE_TAU)
        lg_hi = lg.astype(BF16)
        lg_lo = (lg - lg_hi.astype(F32)).astype(BF16)
        fac = jnp.exp2(_dot(sums_ref[...], jnp.concatenate([lg_hi, lg_lo], axis=0)))
        q = q_ref[:, ks].astype(F32) * q_scale
        k = k_ref[:, ks].astype(F32)
        v = v_ref[:, vs]
        scores = masks_ref[0] * _dot_nt(q.astype(BF16), k.astype(BF16))
        for lev in range(n_lev):
            f = fac[lev * c:(lev + 1) * c]
            scores = scores + masks_ref[lev + 1] * _dot_nt((q * f).astype(BF16), (k * f).astype(BF16))
        f_cum = fac[n_lev * c:(n_lev + 1) * c]
        f_dec = fac[(n_lev + 1) * c:(n_lev + 2) * c]
        state_t = st_ref[hd]
        o = _dot(scores.astype(BF16), v) + _dot_nt((q * f_cum).astype(BF16), state_t.astype(BF16))
        st_ref[hd] = state_t * f_cum[c - 1:c, :] + _dot_tn(v, (k * f_dec).astype(BF16))
        o = o * lax.rsqrt(jnp.mean(o * o, -1, keepdims=True) + RMS_EPS) * ng_ref[...]
        r = r_ref[:, vs].astype(F32)
        o_ref[:, vs] = (o * _silu(r)).astype(o_ref.dtype)


def _gla(qkvr, g_low, w_g2, b_g2, norm_g, *, heads, dk, dv, chunk):
    rows = qkvr.shape[0]
    qk, vd = heads * dk, heads * dv
    assert rows % chunk == 0 and qkvr.shape[1] == 2 * qk + 2 * vd and vd == 2 * qk
    n_lev, sums, masks = _gla_tables(chunk)
    rank_pad = g_low.shape[1]
    return pl.pallas_call(
        functools.partial(_gla_kernel, heads=heads, dk=dk, dv=dv, n_lev=n_lev, q_scale=dk ** -0.5),
        grid=(rows // chunk,),
        in_specs=[
            pl.BlockSpec((chunk, qk), lambda i: (i, 0)),
            pl.BlockSpec((chunk, qk), lambda i: (i, 1)),
            pl.BlockSpec((chunk, vd), lambda i: (i, 1)),
            pl.BlockSpec((chunk, vd), lambda i: (i, 2)),
            pl.BlockSpec((chunk, rank_pad), lambda i: (i, 0)),
            pl.BlockSpec((rank_pad, qk), lambda i: (0, 0)),
            pl.BlockSpec((1, qk), lambda i: (0, 0)),
            pl.BlockSpec((1, dv), lambda i: (0, 0)),
            pl.BlockSpec(sums.shape, lambda i: (0, 0)),
            pl.BlockSpec(masks.shape, lambda i: (0, 0, 0)),
        ],
        out_specs=pl.BlockSpec((chunk, vd), lambda i: (i, 0)),
        out_shape=jax.ShapeDtypeStruct((rows, vd), BF16),
        scratch_shapes=[pltpu.VMEM((heads, dv, dk), F32)],
        compiler_params=_params("arbitrary"),
        name="gla",
    )(qkvr, qkvr, qkvr, qkvr, g_low, w_g2, b_g2.reshape(1, qk), norm_g.reshape(1, dv), sums, masks)


def _sb_tables(tk):
    j = np.arange(tk)[:, None]
    s = np.arange(tk)[None, :]
    tab = j >= s
    return jnp.asarray(-tab.astype(np.float32), BF16)


def _sb_tile(q, k_tile, v_tile, tab, carry, acc, *, diagonal):
    blk = q.shape[0]
    z = _dot_nt(q, k_tile)
    softplus = jnp.maximum(z, 0.0) + jnp.log(1.0 + jnp.exp(-jnp.abs(z)))
    if diagonal:
        strict = (lax.broadcasted_iota(jnp.int32, (blk, blk), 1)
                  < lax.broadcasted_iota(jnp.int32, (blk, blk), 0))
        softplus = jnp.where(strict, softplus, 0.0)
    sums = _dot(softplus.astype(BF16), tab)
    logit = z + sums
    if carry is not None:
        logit = logit + jnp.concatenate([carry] * (blk // LANES), axis=1)
    w = jnp.exp(logit)
    if diagonal:
        w = jnp.where(strict, w, 0.0)
    out = _dot(w.astype(BF16), v_tile)
    total = jnp.broadcast_to(sums[:, :1], (blk, LANES))
    return (total if carry is None else carry + total), (out if acc is None else acc + out)


def _sb_kernel(q_ref, k_ref, v_ref, tab_ref, o_ref, acc_ref, carry_ref, *, blk, n_sub, n_head, dh):
    base = pl.program_id(1) * n_sub
    tab = tab_ref[...]
    units = [(hd, s) for hd in range(n_head) for s in range(n_sub)]

    def rows_of(s):
        return pl.ds(s * blk, blk)

    def cols_of(hd):
        return slice(hd * dh, (hd + 1) * dh)

    def carry_rows(hd, s):
        return pl.ds((hd * n_sub + s) * blk, blk)

    def keys_of(j):
        return pl.ds(pl.multiple_of(j * blk, blk), blk)

    for hd, s in units:
        q = q_ref[rows_of(s), cols_of(hd)]
        carry, acc = _sb_tile(q, k_ref[keys_of(base + s), cols_of(hd)], v_ref[keys_of(base + s), cols_of(hd)],
                              tab, None, None, diagonal=True)
        below = jnp.maximum(base + s - 1, 0)
        carry2, acc2 = _sb_tile(q, k_ref[keys_of(below), cols_of(hd)], v_ref[keys_of(below), cols_of(hd)], tab,
                                carry, acc, diagonal=False)
        if s == 0:
            carry2 = jnp.where(base > 0, carry2, carry)
            acc2 = jnp.where(base > 0, acc2, acc)
        carry_ref[carry_rows(hd, s), :] = carry2
        acc_ref[rows_of(s), cols_of(hd)] = acc2

    def walk(hd, s, j):
        carry, acc = _sb_tile(q_ref[rows_of(s), cols_of(hd)], k_ref[keys_of(j), cols_of(hd)],
                              v_ref[keys_of(j), cols_of(hd)], tab,
                              carry_ref[carry_rows(hd, s), :], acc_ref[rows_of(s), cols_of(hd)], diagonal=False)
        carry_ref[carry_rows(hd, s), :] = carry
        acc_ref[rows_of(s), cols_of(hd)] = acc

    tops = [jnp.max(carry_ref[carry_rows(hd, s), :]) for hd, s in units]
    for (hd, s), top in zip(units, tops):
        def cond(state):
            j, top = state
            return jnp.logical_and(j >= 0, top > SB_LOG_WEIGHT_FLOOR)

        def body(state, hd=hd, s=s):
            j, _ = state
            walk(hd, s, j)
            return j - 1, jnp.max(carry_ref[carry_rows(hd, s), :])

        lax.while_loop(cond, body, (base + s - 2, top))
    o_ref[...] = acc_ref[...].astype(o_ref.dtype)


def _sb_attention(q, kv, *, heads, blk, n_sub):
    rows, d = q.shape
    dh = d // heads
    step = blk * n_sub
    n_head = SB_HEADS_PER_STEP
    width = n_head * dh
    groups = heads // n_head
    assert rows % step == 0 and dh % LANES == 0 and blk % LANES == 0 and kv.shape == (rows, 2 * d)
    assert heads % n_head == 0
    tab = _sb_tables(blk)
    return pl.pallas_call(
        functools.partial(_sb_kernel, blk=blk, n_sub=n_sub, n_head=n_head, dh=dh),
        grid=(groups, rows // step),
        in_specs=[
            pl.BlockSpec((step, width), lambda h, i: (i, h)),
            pl.BlockSpec((rows, width), lambda h, i: (0, h)),
            pl.BlockSpec((rows, width), lambda h, i: (0, h + groups)),
            pl.BlockSpec(tab.shape, lambda h, i: (0, 0)),
        ],
        out_specs=pl.BlockSpec((step, width), lambda h, i: (i, h)),
        out_shape=jax.ShapeDtypeStruct((rows, d), BF16),
        scratch_shapes=[pltpu.VMEM((step, width), F32), pltpu.VMEM((n_head * step, LANES), F32)],
        compiler_params=_params("parallel", "arbitrary"),
        name="sb_attention",
    )(q, kv, kv, tab)


def _tiles(rows_padded):
    tm = ROW_TILE if rows_padded % ROW_TILE == 0 else SB_BLOCK
    return dict(tm=tm)


def kernel(x, meta_tokens, ffn_w_gu, ffn_w_down, ln_g, ln_b, gla_w_in, gla_w_g2, gla_b_g2, gla_norm_g,
           gla_w_out, sb_w_kv, sb_w_q, sb_w_out):
    batch, seq, d = x.shape
    assert batch == 1 and meta_tokens.shape == (N_META, d)
    depth = ffn_w_gu.shape[0]
    n_a = gla_w_in.shape[0]
    alpha = (2.0 * depth) ** 0.25
    rows = N_META + seq
    rows_p = -(-rows // ROW_ALIGN) * ROW_ALIGN
    tm = _tiles(rows_p)["tm"]

    qk = gla_w_g2.shape[2]
    vd = gla_w_out.shape[1]
    dk, dv = qk // GLA_HEADS, vd // GLA_HEADS
    dh = d // SB_HEADS

    h = jnp.concatenate([meta_tokens.astype(F32), x[0], jnp.zeros((rows_p - rows, d), F32)], axis=0)
    hb = h.astype(BF16)
    kv_shared = None

    w_gu_all = ffn_w_gu.astype(BF16)
    w_down_all = ffn_w_down.astype(BF16)

    def ffn(h, layer, half):
        return _ffn_ln(h, w_gu_all, w_down_all, (layer, half), ln_g[layer, 2 * half], ln_b[layer, 2 * half],
                       alpha=alpha, tm=tm, th=FFN_HIDDEN_TILE)

    for layer in range(depth):
        h, hb = ffn(h, layer, 0)
        if layer < n_a:
            w_in = gla_w_in[layer].astype(BF16)
            qkvr = _proj(hb, w_in, out_dtype=BF16, tm=tm, tn=PROJ_COL_TILE, n=2 * qk + 2 * vd)
            w_gate = jnp.pad(w_in[:, 2 * qk + 2 * vd:], ((0, 0), (0, LANES - GLA_GATE_RANK)))
            g_low = _proj(hb, w_gate, out_dtype=F32, tm=tm, tn=LANES)
            w_g2 = jnp.pad(gla_w_g2[layer].astype(BF16), ((0, LANES - GLA_GATE_RANK), (0, 0)))
            mixed = _gla(qkvr, g_low, w_g2, gla_b_g2[layer], gla_norm_g[layer],
                         heads=GLA_HEADS, dk=dk, dv=dv, chunk=GLA_CHUNK)
            w_out = gla_w_out[layer].astype(BF16)
        else:
            jb = layer - n_a
            q = _proj(hb, sb_w_q[jb].astype(BF16), out_dtype=BF16, tm=tm, tn=PROJ_COL_TILE, scale=dh ** -0.5)
            mixed = _sb_attention(q, kv_shared, heads=SB_HEADS, blk=SB_BLOCK, n_sub=SB_GROUP)
            w_out = sb_w_out[jb].astype(BF16)
        if layer == depth - 1 and layer != n_a - 1 and seq % FINAL_ROW_TILE == 0:
            tm = FINAL_ROW_TILE
            h, hb = _mix_ln(mixed, w_out, h, ln_g[layer, 1], ln_b[layer, 1], alpha=alpha, tm=tm,
                            first_row=N_META, rows=seq)
            h, hb = ffn(h, layer, 1)
            return h[None]
        h, hb = _mix_ln(mixed, w_out, h, ln_g[layer, 1], ln_b[layer, 1], alpha=alpha, tm=tm)
        h, hb = ffn(h, layer, 1)
        if layer == n_a - 1:
            kv_shared = _proj(hb, sb_w_kv.astype(BF16), out_dtype=BF16, tm=tm, tn=PROJ_COL_TILE)
    return h[N_META:rows][None]
```
